```python
import math
import jax, jax.numpy as jnp
from jax import lax
import numpy as np

D_MODEL = 2048
BATCH = 1
SEQ = 8192
DEPTH = 2

PLE_DIM = 256
BLOCK_Q = 128
NORM_EPS = 1e-6
N_BRANCH = 3
BRANCH_W = D_MODEL // 2

SSM_GROUP = 16
SSM_STATE = 64
SSM_GROUPS = BRANCH_W // SSM_GROUP
DT_MIN = 1e-3
DT_MAX = 1e-1

MLA_NOPE = 128
MLA_ROPE = 64
MLA_V = 128
MLA_HEADS = BRANCH_W // MLA_V
MLA_Q_RANK = D_MODEL // 4
MLA_KV_RANK = D_MODEL // 4
ROPE_THETA = 10000.0

SB_HEAD_DIM = 128
SB_HEADS = BRANCH_W // SB_HEAD_DIM

N_SSM_IN = BRANCH_W
N_SB_IN = 3 * BRANCH_W
N_GATE_PATH = N_BRANCH * BRANCH_W
N_MERGE = N_BRANCH * D_MODEL
SPLIT_SIZES = (N_SSM_IN, MLA_Q_RANK, MLA_KV_RANK, MLA_ROPE, N_SB_IN, N_GATE_PATH, N_MERGE)
N_IN = N_SSM_IN + MLA_Q_RANK + MLA_KV_RANK + MLA_ROPE + N_SB_IN + N_GATE_PATH + N_MERGE

kernel_name = "hybrid_s5_mla_stickbreaking_gated_merge"


def rms_norm(x, g):
    xf = x.astype(jnp.float32)
    y = xf * lax.rsqrt(jnp.mean(xf * xf, axis=-1, keepdims=True) + NORM_EPS)
    return (y * g.astype(jnp.float32)).astype(x.dtype)


def apply_rope(x, cos, sin):
    half = x.shape[-1] // 2
    xf = x.astype(jnp.float32)
    x1, x2 = xf[..., :half], xf[..., half:]
    shape = (1, x.shape[1]) + (1,) * (x.ndim - 3) + (half,)
    c = cos.reshape(shape)
    s = sin.reshape(shape)
    return jnp.concatenate([x1 * c - x2 * s, x1 * s + x2 * c], axis=-1).astype(x.dtype)


def s5_mixer(u, lam_re, lam_im, log_dt, b_re, b_im, c_re, c_im, d_skip, w_glu):
    bsz, seqlen, _ = u.shape
    f32 = jnp.float32
    uf = u.astype(f32).reshape(bsz, seqlen, SSM_GROUPS, SSM_GROUP)
    lr = lam_re.astype(f32)
    li = lam_im.astype(f32)
    dt = jnp.exp(log_dt.astype(f32))[:, None]
    mag = jnp.exp(lr * dt)
    ab_re = mag * jnp.cos(li * dt)
    ab_im = mag * jnp.sin(li * dt)
    den = lr * lr + li * li
    nr = ab_re - 1.0
    ni = ab_im
    coef_re = (nr * lr + ni * li) / den
    coef_im = (ni * lr - nr * li) / den
    br = b_re.astype(f32)
    bi = b_im.astype(f32)
    bb_re = coef_re[..., None] * br - coef_im[..., None] * bi
    bb_im = coef_re[..., None] * bi + coef_im[..., None] * br
    bu_re = jnp.einsum('bsgh,gph->bsgp', uf, bb_re)
    bu_im = jnp.einsum('bsgh,gph->bsgp', uf, bb_im)
    a_re = jnp.broadcast_to(ab_re, bu_re.shape)
    a_im = jnp.broadcast_to(ab_im, bu_im.shape)

    def combine(e1, e2):
        a1r, a1i, b1r, b1i = e1
        a2r, a2i, b2r, b2i = e2
        return (a1r * a2r - a1i * a2i,
                a1r * a2i + a1i * a2r,
                a2r * b1r - a2i * b1i + b2r,
                a2r * b1i + a2i * b1r + b2i)

    _, _, h_re, h_im = lax.associative_scan(combine, (a_re, a_im, bu_re, bu_im), axis=1)
    y = (jnp.einsum('bsgp,ghp->bsgh', h_re, c_re.astype(f32))
         - jnp.einsum('bsgp,ghp->bsgh', h_im, c_im.astype(f32))
         + d_skip.astype(f32).reshape(SSM_GROUPS, SSM_GROUP) * uf)
    y = jax.nn.gelu(y.reshape(bsz, seqlen, BRANCH_W))
    y = y * jax.nn.sigmoid(y @ w_glu.astype(f32))
    return y.astype(u.dtype)


def mla_mixer(q_lat, kv_lat, k_rope_raw, g_q, g_kv, w_uq, w_ukv, cos, sin):
    bsz, seqlen, _ = q_lat.shape
    q = (rms_norm(q_lat, g_q) @ w_uq).reshape(bsz, seqlen, MLA_HEADS, MLA_NOPE + MLA_ROPE)
    q_nope = q[..., :MLA_NOPE]
    q_rope = apply_rope(q[..., MLA_NOPE:], cos, sin)
    kv = (rms_norm(kv_lat, g_kv) @ w_ukv).reshape(bsz, seqlen, MLA_HEADS, MLA_NOPE + MLA_V)
    k_nope = kv[..., :MLA_NOPE]
    v = kv[..., MLA_NOPE:]
    k_rope = apply_rope(k_rope_raw, cos, sin)
    scale = (MLA_NOPE + MLA_ROPE) ** -0.5
    outs = []
    for blk in range(seqlen // BLOCK_Q):
        q0, q1 = blk * BLOCK_Q, (blk + 1) * BLOCK_Q
        s = (jnp.einsum('bqhd,bkhd->bhqk', q_nope[:, q0:q1], k_nope[:, :q1])
             + jnp.einsum('bqhr,bkr->bhqk', q_rope[:, q0:q1], k_rope[:, :q1]))
        s = s.astype(jnp.float32) * scale
        mask = jnp.arange(q1)[None, :] <= (q0 + jnp.arange(BLOCK_Q))[:, None]
        s = jnp.where(mask, s, -jnp.inf)
        pr = jax.nn.softmax(s, axis=-1).astype(v.dtype)
        outs.append(jnp.einsum('bhqk,bkhd->bqhd', pr, v[:, :q1]))
    o = jnp.concatenate(outs, axis=1)
    return o.reshape(bsz, seqlen, MLA_HEADS * MLA_V)


def stick_breaking_mixer(q, k, v):
    bsz, seqlen = q.shape[0], q.shape[1]
    scale = SB_HEAD_DIM ** -0.5
    outs = []
    for blk in range(seqlen // BLOCK_Q):
        q0, q1 = blk * BLOCK_Q, (blk + 1) * BLOCK_Q
        z = jnp.einsum('bqhd,bkhd->bhqk', q[:, q0:q1], k[:, :q1]).astype(jnp.float32) * scale
        mask = jnp.arange(q1)[None, :] < (q0 + jnp.arange(BLOCK_Q))[:, None]
        log_beta = jax.nn.log_sigmoid(z)
        log_rest = jnp.where(mask, jax.nn.log_sigmoid(-z), 0.0)
        later = lax.cumsum(log_rest, axis=3, reverse=True) - log_rest
        w = jnp.where(mask, jnp.exp(log_beta + later), 0.0).astype(v.dtype)
        outs.append(jnp.einsum('bhqk,bkhd->bqhd', w, v[:, :q1]))
    o = jnp.concatenate(outs, axis=1)
    return o.reshape(bsz, seqlen, SB_HEADS * SB_HEAD_DIM)


def hybrid_layer(x, p_i, ln_g, w_in, lam_re, lam_im, log_dt, b_re, b_im, c_re, c_im, d_skip, w_glu,
                 g_q, g_kv, w_uq, w_ukv, w_branch, w_out, ple_g, w_ple_gate, w_ple_proj, cos, sin):
    bsz, seqlen, _ = x.shape
    h = rms_norm(x, ln_g)
    proj = h @ w_in
    idx = []
    acc = 0
    for size in SPLIT_SIZES[:-1]:
        acc += size
        idx.append(acc)
    u_ssm, q_lat, kv_lat, k_rope_raw, qkv_sb, gate_path, merge_logits = jnp.split(proj, idx, axis=-1)

    y_ssm = s5_mixer(u_ssm, lam_re, lam_im, log_dt, b_re, b_im, c_re, c_im, d_skip, w_glu)
    y_mla = mla_mixer(q_lat, kv_lat, k_rope_raw, g_q, g_kv, w_uq, w_ukv, cos, sin)
    q_sb, k_sb, v_sb = jnp.split(qkv_sb.reshape(bsz, seqlen, 3, SB_HEADS, SB_HEAD_DIM), 3, axis=2)
    y_sb = stick_breaking_mixer(q_sb[:, :, 0], k_sb[:, :, 0], v_sb[:, :, 0])

    ys = jnp.stack([y_ssm, y_mla, y_sb], axis=2)
    ys = ys * jax.nn.silu(gate_path.reshape(bsz, seqlen, N_BRANCH, BRANCH_W))
    branch_out = jnp.einsum('bsnw,nwd->bsnd', ys, w_branch)
    merge_gate = jax.nn.sigmoid(merge_logits.reshape(bsz, seqlen, N_BRANCH, D_MODEL))
    merged = jnp.sum(merge_gate * branch_out, axis=2)
    x = x + merged @ w_out

    e = p_i @ w_ple_proj
    g = jax.nn.sigmoid(rms_norm(x, ple_g) @ w_ple_gate)
    return x + g * e


def setup_inputs(seed: int = 0) -> dict:
    key = jax.random.key(seed)
    ks = jax.random.split(key, 23)
    f32 = jnp.float32

    def nrm(k, shape, scale):
        return jax.random.normal(k, shape, f32) * scale

    def gain(k, shape):
        return 1.0 + nrm(k, shape, 0.02)

    G, P, HG, W = SSM_GROUPS, SSM_STATE, SSM_GROUP, BRANCH_W
    n_idx = jnp.arange(P, dtype=f32)
    return {
        'x': nrm(ks[0], (BATCH, SEQ, D_MODEL), 1.0),
        'p': nrm(ks[1], (DEPTH, BATCH, SEQ, PLE_DIM), 1.0),
        'ln_g': gain(ks[2], (DEPTH, D_MODEL)),
        'w_in': nrm(ks[3], (DEPTH, D_MODEL, N_IN), D_MODEL ** -0.5),
        'ssm_lam_re': -0.5 + nrm(ks[4], (DEPTH, G, P), 0.01),
        'ssm_lam_im': math.pi * n_idx + nrm(ks[5], (DEPTH, G, P), 0.01),
        'ssm_log_dt': jax.random.uniform(ks[6], (DEPTH, G), f32, math.log(DT_MIN), math.log(DT_MAX)),
        'ssm_b_re': nrm(ks[7], (DEPTH, G, P, HG), (2.0 * HG) ** -0.5),
        'ssm_b_im': nrm(ks[8], (DEPTH, G, P, HG), (2.0 * HG) ** -0.5),
        'ssm_c_re': nrm(ks[9], (DEPTH, G, HG, P), (2.0 * P) ** -0.5),
        'ssm_c_im': nrm(ks[10], (DEPTH, G, HG, P), (2.0 * P) ** -0.5),
        'ssm_d': nrm(ks[11], (DEPTH, W), 1.0),
        'ssm_w_glu': nrm(ks[12], (DEPTH, W, W), W ** -0.5),
        'mla_g_q': gain(ks[13], (DEPTH, MLA_Q_RANK)),
        'mla_g_kv': gain(ks[14], (DEPTH, MLA_KV_RANK)),
        'mla_w_uq': nrm(ks[15], (DEPTH, MLA_Q_RANK, MLA_HEADS * (MLA_NOPE + MLA_ROPE)), MLA_Q_RANK ** -0.5),
        'mla_w_ukv': nrm(ks[16], (DEPTH, MLA_KV_RANK, MLA_HEADS * (MLA_NOPE + MLA_V)), MLA_KV_RANK ** -0.5),
        'w_branch': nrm(ks[17], (DEPTH, N_BRANCH, W, D_MODEL), W ** -0.5),
        'w_out': nrm(ks[18], (DEPTH, D_MODEL, D_MODEL), D_MODEL ** -0.5),
        'ple_g': gain(ks[19], (DEPTH, D_MODEL)),
        'w_ple_gate': nrm(ks[20], (DEPTH, D_MODEL, D_MODEL), D_MODEL ** -0.5),
        'w_ple_proj': nrm(ks[21], (DEPTH, PLE_DIM, D_MODEL), PLE_DIM ** -0.5),
        'final_g': gain(ks[22], (D_MODEL,)),
    }


def reference(x, p, ln_g, w_in, ssm_lam_re, ssm_lam_im, ssm_log_dt, ssm_b_re, ssm_b_im, ssm_c_re, ssm_c_im,
              ssm_d, ssm_w_glu, mla_g_q, mla_g_kv, mla_w_uq, mla_w_ukv, w_branch, w_out, ple_g, w_ple_gate,
              w_ple_proj, final_g):
    seqlen = x.shape[1]
    pos = jnp.arange(seqlen, dtype=jnp.float32)
    inv_freq = ROPE_THETA ** (-jnp.arange(0, MLA_ROPE, 2, dtype=jnp.float32) / MLA_ROPE)
    ang = pos[:, None] * inv_freq[None, :]
    cos, sin = jnp.cos(ang), jnp.sin(ang)
    for i in range(DEPTH):
        x = hybrid_layer(x, p[i], ln_g[i], w_in[i], ssm_lam_re[i], ssm_lam_im[i], ssm_log_dt[i],
                         ssm_b_re[i], ssm_b_im[i], ssm_c_re[i], ssm_c_im[i], ssm_d[i], ssm_w_glu[i],
                         mla_g_q[i], mla_g_kv[i], mla_w_uq[i], mla_w_ukv[i], w_branch[i], w_out[i],
                         ple_g[i], w_ple_gate[i], w_ple_proj[i], cos, sin)
    return rms_norm(x, final_g)
```

```python
import functools
import math

import jax
import jax.numpy as jnp
from jax import lax
from jax.experimental import pallas as pl
from jax.experimental.pallas import tpu as pltpu

F32 = jnp.float32
BF16 = jnp.bfloat16

NORM_EPS = 1e-6
ROPE_THETA = 10000.0
DT_GROUP = 16
SSM_STATE = 64
MLA_NOPE = 128
MLA_ROPE = 64
MLA_V = 128
MLA_QK = 256
SB_HEAD_DIM = 128
N_BRANCH = 3

LANES = 128
SUBLANES = 8
VMEM_LIMIT_BYTES = 56 * 1024 * 1024

SB_LOG_WEIGHT_FLOOR = -104.0


def _params(*sem):
    return pltpu.CompilerParams(dimension_semantics=sem, vmem_limit_bytes=VMEM_LIMIT_BYTES)


def _sigmoid(x):
    return 1.0 / (1.0 + jnp.exp(-x))


def _rmsnorm_kernel(x_ref, g_ref, o_ref):
    x = x_ref[...].astype(F32)
    ms = jnp.mean(x * x, axis=-1, keepdims=True)
    o_ref[...] = (x * lax.rsqrt(ms + NORM_EPS) * g_ref[...]).astype(o_ref.dtype)


def rmsnorm(x, g, out_dtype, tm=512):
    m, d = x.shape
    return pl.pallas_call(
        _rmsnorm_kernel,
        grid=(m // tm,),
        in_specs=[pl.BlockSpec((tm, d), lambda i: (i, 0)),
                  pl.BlockSpec((1, d), lambda i: (0, 0))],
        out_specs=pl.BlockSpec((tm, d), lambda i: (i, 0)),
        out_shape=jax.ShapeDtypeStruct((m, d), out_dtype),
        compiler_params=_params("parallel"),
        name="rmsnorm",
    )(x, g.reshape(1, d).astype(F32))


def _matmul_kernel(a_ref, w_ref, o_ref):
    o_ref[...] = jnp.dot(a_ref[...], w_ref[...], preferred_element_type=F32).astype(o_ref.dtype)


def matmul(a, w, out_dtype, tm, tn, name):
    m, k = a.shape
    n = w.shape[1]
    return pl.pallas_call(
        _matmul_kernel,
        grid=(m // tm, n // tn),
        in_specs=[pl.BlockSpec((tm, k), lambda i, j: (i, 0)),
                  pl.BlockSpec((k, tn), lambda i, j: (0, j))],
        out_specs=pl.BlockSpec((tm, tn), lambda i, j: (i, j)),
        out_shape=jax.ShapeDtypeStruct((m, n), out_dtype),
        compiler_params=_params("parallel", "arbitrary"),
        name=name,
    )(a, w)


def _matmul_residual_kernel(a_ref, w_ref, r_ref, o_ref):
    o_ref[...] = r_ref[...] + jnp.dot(a_ref[...], w_ref[...], preferred_element_type=F32)


def matmul_residual(a, w, res, tm, tn, name):
    m, k = a.shape
    n = w.shape[1]
    return pl.pallas_call(
        _matmul_residual_kernel,
        grid=(m // tm, n // tn),
        in_specs=[pl.BlockSpec((tm, k), lambda i, j: (i, 0)),
                  pl.BlockSpec((k, tn), lambda i, j: (0, j)),
                  pl.BlockSpec((tm, tn), lambda i, j: (i, j))],
        out_specs=pl.BlockSpec((tm, tn), lambda i, j: (i, j)),
        out_shape=jax.ShapeDtypeStruct((m, n), F32),
        compiler_params=_params("parallel", "arbitrary"),
        name=name,
    )(a, w, res)


def _s5_scan_kernel(u_ref, win_ref, pw_ref, cmat_ref, d_ref, y_ref, h_ref, carry_ref, *, n_tiles, half):
    t = pl.program_id(1)

    @pl.when(t == 0)
    def _():
        carry_ref[...] = jnp.zeros_like(carry_ref)

    u = u_ref[...]
    h_ref[...] = jnp.dot(u, win_ref[0], preferred_element_type=F32)

    def tile(i, carry):
        c_re, c_im = carry
        rows = pl.ds(pl.multiple_of(i * SUBLANES, SUBLANES), SUBLANES)
        x_re = h_ref[rows, :half]
        x_im = h_ref[rows, half:]
        for lvl, shift in enumerate((1, 2, 4)):
            a_re = pw_ref[0, (2 * lvl) * SUBLANES:(2 * lvl + 1) * SUBLANES, :]
            a_im = pw_ref[0, (2 * lvl + 1) * SUBLANES:(2 * lvl + 2) * SUBLANES, :]
            s_re = pltpu.roll(x_re, shift, 0)
            s_im = pltpu.roll(x_im, shift, 0)
            x_re, x_im = (x_re + (a_re * s_re - a_im * s_im),
                          x_im + (a_re * s_im + a_im * s_re))
        p_re = pw_ref[0, 6 * SUBLANES:7 * SUBLANES, :]
        p_im = pw_ref[0, 7 * SUBLANES:8 * SUBLANES, :]
        x_re, x_im = (x_re + (p_re * c_re - p_im * c_im),
                      x_im + (p_re * c_im + p_im * c_re))
        h_ref[rows, :half] = x_re
        h_ref[rows, half:] = x_im
        last = SUBLANES - 1
        return (jnp.broadcast_to(x_re[last:, :], x_re.shape),
                jnp.broadcast_to(x_im[last:, :], x_im.shape))

    c_re, c_im = lax.fori_loop(0, n_tiles, tile, (carry_ref[0], carry_ref[1]), unroll=4)
    carry_ref[0] = c_re
    carry_ref[1] = c_im

    y = jnp.dot(h_ref[...].astype(BF16), cmat_ref[0], preferred_element_type=F32)
    y = y + d_ref[...] * u.astype(F32)
    c0 = math.sqrt(2.0 / math.pi)
    y = 0.5 * y * (1.0 + jnp.tanh(c0 * (y + 0.044715 * (y * y * y))))
    y_ref[...] = y.astype(y_ref.dtype)


def s5_scan(lat, win, pw, cmat, d, tc=512):
    s = lat.shape[0]
    nblk, _, two_half = win.shape
    half = two_half // 2
    w = nblk * LANES
    kern = functools.partial(_s5_scan_kernel, n_tiles=tc // SUBLANES, half=half)
    return pl.pallas_call(
        kern,
        grid=(nblk, s // tc),
        in_specs=[pl.BlockSpec((tc, LANES), lambda b, t: (t, b)),
                  pl.BlockSpec((1, LANES, two_half), lambda b, t: (b, 0, 0)),
                  pl.BlockSpec((1, 8 * SUBLANES, half), lambda b, t: (b, 0, 0)),
                  pl.BlockSpec((1, two_half, LANES), lambda b, t: (b, 0, 0)),
                  pl.BlockSpec((1, LANES), lambda b, t: (0, b))],
        out_specs=pl.BlockSpec((tc, LANES), lambda b, t: (t, b)),
        out_shape=jax.ShapeDtypeStruct((s, w), BF16),
        scratch_shapes=[pltpu.VMEM((tc, two_half), F32),
                        pltpu.VMEM((2, SUBLANES, half), F32)],
        compiler_params=_params("parallel", "arbitrary"),
        name="s5_scan",
    )(lat, win, pw, cmat, d)


def _s5_tables(lam_re, lam_im, log_dt, b_re, b_im, c_re, c_im, d_skip):
    g, p = lam_re.shape
    hg = b_re.shape[-1]
    gpb = LANES // hg
    nblk = g // gpb
    lr = lam_re.astype(F32)
    li = lam_im.astype(F32)
    dt = jnp.exp(log_dt.astype(F32))[:, None]
    mag = jnp.exp(lr * dt)
    ab_re = mag * jnp.cos(li * dt)
    ab_im = mag * jnp.sin(li * dt)
    den = lr * lr + li * li
    nr = ab_re - 1.0
    ni = ab_im
    coef_re = (nr * lr + ni * li) / den
    coef_im = (ni * lr - nr * li) / den
    br = b_re.astype(F32)
    bi = b_im.astype(F32)
    bb_re = coef_re[..., None] * br - coef_im[..., None] * bi
    bb_im = coef_re[..., None] * bi + coef_im[..., None] * br
    eye = jnp.eye(gpb, dtype=F32)

    def block_diag_in(bb):
        v = jnp.transpose(bb, (0, 2, 1)).reshape(nblk, gpb, hg, p)
        return jnp.einsum("bghp,gk->bghkp", v, eye).reshape(nblk, gpb * hg, gpb * p)

    win = jnp.concatenate([block_diag_in(bb_re), block_diag_in(bb_im)], axis=-1).astype(BF16)

    def block_diag_out(c):
        v = jnp.transpose(c.astype(F32), (0, 2, 1)).reshape(nblk, gpb, p, hg)
        return jnp.einsum("bgph,gk->bgpkh", v, eye).reshape(nblk, gpb * p, gpb * hg)

    cmat = jnp.concatenate([block_diag_out(c_re), -block_diag_out(c_im)], axis=1).astype(BF16)

    pows = [(ab_re, ab_im)]
    for _ in range(SUBLANES - 1):
        qr, qi = pows[-1]
        pows.append((qr * ab_re - qi * ab_im, qr * ab_im + qi * ab_re))
    row = jnp.arange(SUBLANES)[:, None]

    def flat(v):
        return v.reshape(nblk, 1, gpb * p)

    parts = []
    for shift in (1, 2, 4):
        keep = (row >= shift).astype(F32)[None]
        parts.append(flat(pows[shift - 1][0]) * keep)
        parts.append(flat(pows[shift - 1][1]) * keep)
    parts.append(jnp.concatenate([flat(pows[r][0]) for r in range(SUBLANES)], axis=1))
    parts.append(jnp.concatenate([flat(pows[r][1]) for r in range(SUBLANES)], axis=1))
    pw = jnp.concatenate(parts, axis=1)
    return win, pw, cmat, d_skip.astype(F32).reshape(1, -1)


def _glu_kernel(y_ref, w_ref, o_ref):
    y = y_ref[...]
    z = jnp.dot(y, w_ref[...], preferred_element_type=F32)
    o_ref[...] = (y.astype(F32) * _sigmoid(z)).astype(o_ref.dtype)


def glu(y, w, tm=512):
    s, n = y.shape
    return pl.pallas_call(
        _glu_kernel,
        grid=(s // tm,),
        in_specs=[pl.BlockSpec((tm, n), lambda i: (i, 0)),
                  pl.BlockSpec((n, n), lambda i: (0, 0))],
        out_specs=pl.BlockSpec((tm, n), lambda i: (i, 0)),
        out_shape=jax.ShapeDtypeStruct((s, n), BF16),
        compiler_params=_params("parallel"),
        name="s5_glu",
    )(y, w)


def _rms(x, g):
    x = x.astype(F32)
    ms = jnp.mean(x * x, axis=-1, keepdims=True)
    return (x * lax.rsqrt(ms + NORM_EPS) * g).astype(BF16)


def _mla_q_kernel(lat_ref, g_ref, w_ref, wrot_ref, cos_ref, sin_ref, q_ref, *, heads, scale):
    h = _rms(lat_ref[...], g_ref[...])
    main = jnp.dot(h, w_ref[...], preferred_element_type=F32)
    rot = jnp.dot(h, wrot_ref[...], preferred_element_type=F32)
    cos = cos_ref[...]
    sin = sin_ref[...]
    for hd in range(heads):
        base = hd * MLA_QK
        q_ref[:, base:base + MLA_NOPE] = (main[:, base:base + MLA_NOPE] * scale).astype(q_ref.dtype)
        r = main[:, base + MLA_NOPE:base + MLA_QK] * cos + rot[:, hd * LANES:(hd + 1) * LANES] * sin
        q_ref[:, base + MLA_NOPE:base + MLA_QK] = (r * scale).astype(q_ref.dtype)


def mla_q_proj(lat, g_q, w_main, w_rot, cos, sin, heads, tm=512):
    s = lat.shape[0]
    rank = w_main.shape[0]
    scale = (MLA_NOPE + MLA_ROPE) ** -0.5
    col_blk = 1024 // rank
    return pl.pallas_call(
        functools.partial(_mla_q_kernel, heads=heads, scale=scale),
        grid=(s // tm,),
        in_specs=[pl.BlockSpec((tm, rank), lambda i: (i, col_blk)),
                  pl.BlockSpec((1, rank), lambda i: (0, 0)),
                  pl.BlockSpec(w_main.shape, lambda i: (0, 0)),
                  pl.BlockSpec(w_rot.shape, lambda i: (0, 0)),
                  pl.BlockSpec((tm, LANES), lambda i: (i, 0)),
                  pl.BlockSpec((tm, LANES), lambda i: (i, 0))],
        out_specs=pl.BlockSpec((tm, heads * MLA_QK), lambda i: (i, 0)),
        out_shape=jax.ShapeDtypeStruct((s, heads * MLA_QK), BF16),
        compiler_params=_params("parallel"),
        name="mla_q_proj",
    )(lat, g_q, w_main, w_rot, cos, sin)


def _mla_kv_kernel(lat_ref, g_ref, wk_ref, wv_ref, kr_ref, cos_ref, sin_ref, k_ref, v_ref, *, heads):
    h = _rms(lat_ref[...], g_ref[...])
    kn = jnp.dot(h, wk_ref[...], preferred_element_type=F32)
    v_ref[...] = jnp.dot(h, wv_ref[...], preferred_element_type=F32).astype(v_ref.dtype)
    kr_in = kr_ref[...].astype(F32)
    kr = (kr_in[:, :LANES] * cos_ref[...] + kr_in[:, LANES:] * sin_ref[...]).astype(k_ref.dtype)
    for hd in range(heads):
        base = hd * MLA_QK
        k_ref[:, base:base + MLA_NOPE] = kn[:, hd * MLA_NOPE:(hd + 1) * MLA_NOPE].astype(k_ref.dtype)
        k_ref[:, base + MLA_NOPE:base + MLA_QK] = kr


def mla_kv_proj(lat, g_kv, w_k, w_v, kr_raw, cos, sin, heads, tm=512):
    s = lat.shape[0]
    rank = w_k.shape[0]
    col_blk = (1024 + rank) // rank
    return pl.pallas_call(
        functools.partial(_mla_kv_kernel, heads=heads),
        grid=(s // tm,),
        in_specs=[pl.BlockSpec((tm, rank), lambda i: (i, col_blk)),
                  pl.BlockSpec((1, rank), lambda i: (0, 0)),
                  pl.BlockSpec(w_k.shape, lambda i: (0, 0)),
                  pl.BlockSpec(w_v.shape, lambda i: (0, 0)),
                  pl.BlockSpec((tm, 2 * LANES), lambda i: (i, 0)),
                  pl.BlockSpec((tm, LANES), lambda i: (i, 0)),
                  pl.BlockSpec((tm, LANES), lambda i: (i, 0))],
        out_specs=[pl.BlockSpec((tm, heads * MLA_QK), lambda i: (i, 0)),
                   pl.BlockSpec((tm, heads * MLA_V), lambda i: (i, 0))],
        out_shape=[jax.ShapeDtypeStruct((s, heads * MLA_QK), BF16),
                   jax.ShapeDtypeStruct((s, heads * MLA_V), BF16)],
        compiler_params=_params("parallel"),
        name="mla_kv_proj",
    )(lat, g_kv, w_k, w_v, kr_raw, cos, sin)


def _mla_attn_kernel(q_ref, k_ref, v_ref, o_ref, *, t):
    i = pl.program_id(1)
    q = q_ref[...]

    def step(j, carry, masked):
        m, l, acc = carry
        rows = pl.ds(pl.multiple_of(j * t, t), t)
        s = lax.dot_general(q, k_ref[rows, :], (((1,), (1,)), ((), ())), preferred_element_type=F32)
        if masked:
            r = lax.broadcasted_iota(jnp.int32, s.shape, 0)
            c = lax.broadcasted_iota(jnp.int32, s.shape, 1)
            s = jnp.where(c <= r, s, -jnp.inf)
        m_new = jnp.maximum(m, jnp.max(s, axis=-1, keepdims=True))
        alpha = jnp.exp(m - m_new)
        p = jnp.exp(s - m_new)
        l = alpha * l + jnp.sum(p, axis=-1, keepdims=True)
        acc = alpha * acc + jnp.dot(p.astype(BF16), v_ref[rows, :], preferred_element_type=F32)
        return m_new, l, acc

    init = (jnp.full((t, 1), -jnp.inf, F32), jnp.zeros((t, 1), F32), jnp.zeros((t, MLA_V), F32))
    carry = lax.fori_loop(0, i, lambda j, c: step(j, c, False), init)
    _, l, acc = step(i, carry, True)
    o_ref[...] = (acc / l).astype(o_ref.dtype)


def mla_attention(q, k, v, heads, t=256):
    s = q.shape[0]
    return pl.pallas_call(
        functools.partial(_mla_attn_kernel, t=t),
        grid=(heads, s // t),
        in_specs=[pl.BlockSpec((t, MLA_QK), lambda h, i: (i, h)),
                  pl.BlockSpec((s, MLA_QK), lambda h, i: (0, h)),
                  pl.BlockSpec((s, MLA_V), lambda h, i: (0, h))],
        out_specs=pl.BlockSpec((t, MLA_V), lambda h, i: (i, h)),
        out_shape=jax.ShapeDtypeStruct((s, heads * MLA_V), BF16),
        compiler_params=_params("parallel", "arbitrary"),
        name="mla_attention",
    )(q, k, v)


def _sb_attn_kernel(q_ref, k_ref, v_ref, tri_ref, o_ref, acc_ref, rest_ref, *, t, scale):
    i = pl.program_id(1)
    q = q_ref[...]
    tri = tri_ref[...]
    acc_ref[...] = jnp.zeros_like(acc_ref)
    rest_ref[...] = jnp.zeros_like(rest_ref)
    r_idx = lax.broadcasted_iota(jnp.int32, (t, t), 0)
    c_idx = lax.broadcasted_iota(jnp.int32, (t, t), 1)

    def cond(carry):
        j, rest_max = carry
        return jnp.logical_and(j >= 0, rest_max > SB_LOG_WEIGHT_FLOOR)

    def body(carry):
        j, _ = carry
        rows = pl.ds(pl.multiple_of(j * t, t), t)
        z = lax.dot_general(q, k_ref[rows, :], (((1,), (1,)), ((), ())), preferred_element_type=F32) * scale
        log_beta = jnp.minimum(z, 0.0) - jnp.log(1.0 + jnp.exp(-jnp.abs(z)))
        mask = (c_idx + j * t) < (r_idx + i * t)
        log_rest = jnp.where(mask, log_beta - z, 0.0)
        hi = log_rest.astype(BF16)
        lo = (log_rest - hi.astype(F32)).astype(BF16)
        later = (jnp.dot(hi, tri, preferred_element_type=F32)
                 + jnp.dot(lo, tri, preferred_element_type=F32)
                 + rest_ref[...])
        w = jnp.where(mask, jnp.exp(log_beta + later), 0.0)
        acc_ref[...] += jnp.dot(w.astype(BF16), v_ref[rows, :], preferred_element_type=F32)
        rest = rest_ref[...] + jnp.sum(log_rest, axis=-1, keepdims=True)
        rest_ref[...] = rest
        return j - 1, jnp.max(rest)

    lax.while_loop(cond, body, (i, jnp.float32(0.0)))
    o_ref[...] = acc_ref[...].astype(o_ref.dtype)


def sb_attention(sbgm, heads, t=256):
    s = sbgm.shape[0]
    tri = (jnp.arange(t)[:, None] > jnp.arange(t)[None, :]).astype(BF16)
    return pl.pallas_call(
        functools.partial(_sb_attn_kernel, t=t, scale=SB_HEAD_DIM ** -0.5),
        grid=(heads, s // t),
        in_specs=[pl.BlockSpec((t, SB_HEAD_DIM), lambda h, i: (i, h)),
                  pl.BlockSpec((s, SB_HEAD_DIM), lambda h, i: (0, heads + h)),
                  pl.BlockSpec((s, SB_HEAD_DIM), lambda h, i: (0, 2 * heads + h)),
                  pl.BlockSpec((t, t), lambda h, i: (0, 0))],
        out_specs=pl.BlockSpec((t, SB_HEAD_DIM), lambda h, i: (i, h)),
        out_shape=jax.ShapeDtypeStruct((s, heads * SB_HEAD_DIM), BF16),
        scratch_shapes=[pltpu.VMEM((t, SB_HEAD_DIM), F32), pltpu.VMEM((t, 1), F32)],
        compiler_params=_params("parallel", "arbitrary"),
        name="sb_attention",
    )(sbgm, sbgm, sbgm, tri)


def _merge_kernel(y0_ref, y1_ref, y2_ref, g0_ref, g1_ref, g2_ref, m0_ref, m1_ref, m2_ref,
                  w_ref, o_ref, ys_ref):
    j = pl.program_id(1)

    @pl.when(j == 0)
    def _():
        for n, (y_ref, g_ref) in enumerate(((y0_ref, g0_ref), (y1_ref, g1_ref), (y2_ref, g2_ref))):
            g = g_ref[...].astype(F32)
            ys_ref[n] = (y_ref[...].astype(F32) * (g * _sigmoid(g))).astype(BF16)

    acc = None
    for n, m_ref in enumerate((m0_ref, m1_ref, m2_ref)):
        b = jnp.dot(ys_ref[n], w_ref[n], preferred_element_type=F32)
        term = _sigmoid(m_ref[...].astype(F32)) * b
        acc = term if acc is None else acc + term
    o_ref[...] = acc.astype(o_ref.dtype)


def gated_merge(ys, sbgm, w_branch, d_model, tm=512, tn=1024):
    s, w = ys[0].shape
    gate_col0 = 3 * w
    merge_col0 = gate_col0 + N_BRANCH * w
    y_spec = pl.BlockSpec((tm, w), lambda i, j: (i, 0))

    def gate_spec(n):
        return pl.BlockSpec((tm, w), lambda i, j: (i, gate_col0 // w + n))

    def merge_spec(n):
        return pl.BlockSpec((tm, tn), lambda i, j: (i, (merge_col0 + n * d_model) // tn + j))

    return pl.pallas_call(
        _merge_kernel,
        grid=(s // tm, d_model // tn),
        in_specs=[y_spec, y_spec, y_spec,
                  gate_spec(0), gate_spec(1), gate_spec(2),
                  merge_spec(0), merge_spec(1), merge_spec(2),
                  pl.BlockSpec((N_BRANCH, w, tn), lambda i, j: (0, 0, j))],
        out_specs=pl.BlockSpec((tm, tn), lambda i, j: (i, j)),
        out_shape=jax.ShapeDtypeStruct((s, d_model), BF16),
        scratch_shapes=[pltpu.VMEM((N_BRANCH, tm, w), BF16)],
        compiler_params=_params("parallel", "arbitrary"),
        name="gated_merge",
    )(ys[0], ys[1], ys[2], sbgm, sbgm, sbgm, sbgm, sbgm, sbgm, w_branch)


def _ple_kernel(h_ref, wg_ref, p_ref, wp_ref, x_ref, o_ref):
    g = _sigmoid(jnp.dot(h_ref[...], wg_ref[...], preferred_element_type=F32))
    e = jnp.dot(p_ref[...].astype(BF16), wp_ref[...], preferred_element_type=F32)
    o_ref[...] = x_ref[...] + g * e


def ple_gate(h, w_gate, p, w_proj, x, tm=1024, tn=1024):
    s, d = x.shape
    pd = p.shape[1]
    return pl.pallas_call(
        _ple_kernel,
        grid=(s // tm, d // tn),
        in_specs=[pl.BlockSpec((tm, d), lambda i, j: (i, 0)),
                  pl.BlockSpec((d, tn), lambda i, j: (0, j)),
                  pl.BlockSpec((tm, pd), lambda i, j: (i, 0)),
                  pl.BlockSpec((pd, tn), lambda i, j: (0, j)),
                  pl.BlockSpec((tm, tn), lambda i, j: (i, j))],
        out_specs=pl.BlockSpec((tm, tn), lambda i, j: (i, j)),
        out_shape=jax.ShapeDtypeStruct((s, d), F32),
        compiler_params=_params("parallel", "arbitrary"),
        name="ple_gate",
    )(h, w_gate, p, w_proj, x)


def _rot_half_cols(w):
    half = MLA_ROPE // 2
    return jnp.concatenate([-w[..., half:], w[..., :half]], axis=-1)


def _mla_weights(w_uq, w_ukv, heads):
    rank = w_uq.shape[0]
    wq = w_uq.reshape(rank, heads, MLA_NOPE + MLA_ROPE)
    zeros = jnp.zeros((rank, heads, MLA_QK - MLA_NOPE - MLA_ROPE), w_uq.dtype)
    w_main = jnp.concatenate([wq, zeros], axis=-1).reshape(rank, heads * MLA_QK).astype(BF16)
    w_rot = jnp.concatenate([_rot_half_cols(wq[..., MLA_NOPE:]), zeros], axis=-1)
    w_rot = w_rot.reshape(rank, heads * LANES).astype(BF16)
    wkv = w_ukv.reshape(w_ukv.shape[0], heads, MLA_NOPE + MLA_V)
    w_k = wkv[..., :MLA_NOPE].reshape(-1, heads * MLA_NOPE).astype(BF16)
    w_v = wkv[..., MLA_NOPE:].reshape(-1, heads * MLA_V).astype(BF16)
    return w_main, w_rot, w_k, w_v


def _layer(x, p_i, ln_g, w_in, s5, w_glu, g_q, g_kv, w_uq, w_ukv, w_branch, w_out, ple_g, w_ple_gate,
           w_ple_proj, cos, sin):
    s, d = x.shape
    w = d // 2
    q_rank = g_q.shape[0]
    kv_rank = g_kv.shape[0]
    heads = w // MLA_V
    n_lat = w + q_rank + kv_rank
    n_rope_end = n_lat + MLA_ROPE

    w_lat = w_in[:, :n_lat].astype(BF16)
    w_kr = w_in[:, n_lat:n_rope_end]
    zeros = jnp.zeros((d, LANES - MLA_ROPE), w_in.dtype)
    w_kr = jnp.concatenate([w_kr, zeros, _rot_half_cols(w_kr), zeros], axis=-1).astype(BF16)
    w_sbgm = w_in[:, n_rope_end:].astype(BF16)

    h = rmsnorm(x, ln_g, BF16)
    lat = matmul(h, w_lat, BF16, tm=1024, tn=1024, name="proj_lat")
    kr_raw = matmul(h, w_kr, BF16, tm=1024, tn=2 * LANES, name="proj_rope")
    sbgm = matmul(h, w_sbgm, BF16, tm=1024, tn=1024, name="proj_sbgm")

    win, pw, cmat, d_skip = s5
    y_ssm = glu(s5_scan(lat, win, pw, cmat, d_skip), w_glu.astype(BF16))

    w_main, w_rot, w_k, w_v = _mla_weights(w_uq, w_ukv, heads)
    q = mla_q_proj(lat, g_q.reshape(1, -1).astype(F32), w_main, w_rot, cos, sin, heads)
    k, v = mla_kv_proj(lat, g_kv.reshape(1, -1).astype(F32), w_k, w_v, kr_raw, cos, sin, heads)
    y_mla = mla_attention(q, k, v, heads)

    y_sb = sb_attention(sbgm, heads)

    merged = gated_merge((y_ssm, y_mla, y_sb), sbgm, w_branch.astype(BF16), d)
    x = matmul_residual(merged, w_out.astype(BF16), x, tm=1024, tn=1024, name="out_proj")

    hp = rmsnorm(x, ple_g, BF16)
    return ple_gate(hp, w_ple_gate.astype(BF16), p_i, w_ple_proj.astype(BF16), x)


def kernel(x, p, ln_g, w_in, ssm_lam_re, ssm_lam_im, ssm_log_dt, ssm_b_re, ssm_b_im, ssm_c_re, ssm_c_im,
           ssm_d, ssm_w_glu, mla_g_q, mla_g_kv, mla_w_uq, mla_w_ukv, w_branch, w_out, ple_g, w_ple_gate,
           w_ple_proj, final_g):
    bsz, seqlen, d = x.shape
    depth = w_in.shape[0]
    pos = jnp.arange(seqlen, dtype=F32)
    inv_freq = ROPE_THETA ** (-jnp.arange(0, MLA_ROPE, 2, dtype=F32) / MLA_ROPE)
    ang = pos[:, None] * inv_freq[None, :]
    pad = jnp.zeros((seqlen, LANES - MLA_ROPE), F32)
    cos = jnp.concatenate([jnp.cos(ang), jnp.cos(ang), pad], axis=-1)
    sin = jnp.concatenate([jnp.sin(ang), jnp.sin(ang), pad], axis=-1)

    outs = []
    for b in range(bsz):
        xb = x[b]
        for i in range(depth):
            s5 = _s5_tables(ssm_lam_re[i], ssm_lam_im[i], ssm_log_dt[i], ssm_b_re[i], ssm_b_im[i],
                            ssm_c_re[i], ssm_c_im[i], ssm_d[i])
            xb = _layer(xb, p[i, b], ln_g[i], w_in[i], s5, ssm_w_glu[i], mla_g_q[i], mla_g_kv[i],
                        mla_w_uq[i], mla_w_ukv[i], w_branch[i], w_out[i], ple_g[i], w_ple_gate[i],
                        w_ple_proj[i], cos, sin)
        outs.append(rmsnorm(xb, final_g, x.dtype))
    return jnp.stack(outs, axis=0)
```

```python
import functools
import math

import jax
import jax.numpy as jnp
from jax import lax
from jax.experimental import pallas as pl
from jax.experimental.pallas import tpu as pltpu

F32 = jnp.float32
BF16 = jnp.bfloat16

NORM_EPS = 1e-6
ROPE_THETA = 10000.0
DT_GROUP = 16
SSM_STATE = 64
MLA_NOPE = 128
MLA_ROPE = 64
MLA_V = 128
MLA_QK = 256
SB_HEAD_DIM = 128
N_BRANCH = 3

LANES = 128
SUBLANES = 8
VMEM_LIMIT_BYTES = 56 * 1024 * 1024

SB_LOG_WEIGHT_FLOOR = -104.0


def _params(*sem):
    return pltpu.CompilerParams(dimension_semantics=sem, vmem_limit_bytes=VMEM_LIMIT_BYTES)


def _sigmoid(x):
    return 0.5 * (1.0 + jnp.tanh(0.5 * x))


def _rmsnorm_kernel(x_ref, g_ref, o_ref):
    x = x_ref[...].astype(F32)
    ms = jnp.mean(x * x, axis=-1, keepdims=True)
    o_ref[...] = (x * lax.rsqrt(ms + NORM_EPS) * g_ref[...]).astype(o_ref.dtype)


def rmsnorm(x, g, out_dtype, tm=512):
    m, d = x.shape
    return pl.pallas_call(
        _rmsnorm_kernel,
        grid=(m // tm,),
        in_specs=[pl.BlockSpec((tm, d), lambda i: (i, 0)),
                  pl.BlockSpec((1, d), lambda i: (0, 0))],
        out_specs=pl.BlockSpec((tm, d), lambda i: (i, 0)),
        out_shape=jax.ShapeDtypeStruct((m, d), out_dtype),
        compiler_params=_params("parallel"),
        name="rmsnorm",
    )(x, g.reshape(1, d).astype(F32))


def _matmul_kernel(a_ref, w_ref, o_ref):
    o_ref[...] = jnp.dot(a_ref[...], w_ref[...], preferred_element_type=F32).astype(o_ref.dtype)


def matmul(a, w, out_dtype, tm, tn, name):
    m, k = a.shape
    n = w.shape[1]
    return pl.pallas_call(
        _matmul_kernel,
        grid=(m // tm, n // tn),
        in_specs=[pl.BlockSpec((tm, k), lambda i, j: (i, 0)),
                  pl.BlockSpec((k, tn), lambda i, j: (0, j))],
        out_specs=pl.BlockSpec((tm, tn), lambda i, j: (i, j)),
        out_shape=jax.ShapeDtypeStruct((m, n), out_dtype),
        compiler_params=_params("parallel", "arbitrary"),
        name=name,
    )(a, w)


def _matmul_residual_kernel(a_ref, w_ref, r_ref, o_ref):
    o_ref[...] = r_ref[...] + jnp.dot(a_ref[...], w_ref[...], preferred_element_type=F32)


def matmul_residual(a, w, res, tm, tn, name):
    m, k = a.shape
    n = w.shape[1]
    return pl.pallas_call(
        _matmul_residual_kernel,
        grid=(m // tm, n // tn),
        in_specs=[pl.BlockSpec((tm, k), lambda i, j: (i, 0)),
                  pl.BlockSpec((k, tn), lambda i, j: (0, j)),
                  pl.BlockSpec((tm, tn), lambda i, j: (i, j))],
        out_specs=pl.BlockSpec((tm, tn), lambda i, j: (i, j)),
        out_shape=jax.ShapeDtypeStruct((m, n), F32),
        compiler_params=_params("parallel", "arbitrary"),
        name=name,
    )(a, w, res)


def _s5_scan_kernel(u_ref, win_ref, pw_ref, cmat_ref, d_ref, y_ref, h_ref, carry_ref, *, n_tiles, half):
    t = pl.program_id(1)

    @pl.when(t == 0)
    def _():
        carry_ref[...] = jnp.zeros_like(carry_ref)

    u = u_ref[...]
    h_ref[...] = jnp.dot(u, win_ref[0], preferred_element_type=F32)

    def tile(i, carry):
        c_re, c_im = carry
        rows = pl.ds(pl.multiple_of(i * SUBLANES, SUBLANES), SUBLANES)
        x_re = h_ref[rows, :half]
        x_im = h_ref[rows, half:]
        for lvl, shift in enumerate((1, 2, 4)):
            a_re = pw_ref[0, (2 * lvl) * SUBLANES:(2 * lvl + 1) * SUBLANES, :]
            a_im = pw_ref[0, (2 * lvl + 1) * SUBLANES:(2 * lvl + 2) * SUBLANES, :]
            s_re = pltpu.roll(x_re, shift, 0)
            s_im = pltpu.roll(x_im, shift, 0)
            x_re, x_im = (x_re + (a_re * s_re - a_im * s_im),
                          x_im + (a_re * s_im + a_im * s_re))
        p_re = pw_ref[0, 6 * SUBLANES:7 * SUBLANES, :]
        p_im = pw_ref[0, 7 * SUBLANES:8 * SUBLANES, :]
        x_re, x_im = (x_re + (p_re * c_re - p_im * c_im),
                      x_im + (p_re * c_im + p_im * c_re))
        h_ref[rows, :half] = x_re
        h_ref[rows, half:] = x_im
        last = SUBLANES - 1
        return (jnp.broadcast_to(x_re[last:, :], x_re.shape),
                jnp.broadcast_to(x_im[last:, :], x_im.shape))

    c_re, c_im = lax.fori_loop(0, n_tiles, tile, (carry_ref[0], carry_ref[1]), unroll=4)
    carry_ref[0] = c_re
    carry_ref[1] = c_im

    y = jnp.dot(h_ref[...].astype(BF16), cmat_ref[0], preferred_element_type=F32)
    y = y + d_ref[...] * u.astype(F32)
    c0 = math.sqrt(2.0 / math.pi)
    y = 0.5 * y * (1.0 + jnp.tanh(c0 * (y + 0.044715 * (y * y * y))))
    y_ref[...] = y.astype(y_ref.dtype)


def s5_scan(lat, win, pw, cmat, d, tc=512):
    s = lat.shape[0]
    nblk, _, two_half = win.shape
    half = two_half // 2
    w = nblk * LANES
    kern = functools.partial(_s5_scan_kernel, n_tiles=tc // SUBLANES, half=half)
    return pl.pallas_call(
        kern,
        grid=(nblk, s // tc),
        in_specs=[pl.BlockSpec((tc, LANES), lambda b, t: (t, b)),
                  pl.BlockSpec((1, LANES, two_half), lambda b, t: (b, 0, 0)),
                  pl.BlockSpec((1, 8 * SUBLANES, half), lambda b, t: (b, 0, 0)),
                  pl.BlockSpec((1, two_half, LANES), lambda b, t: (b, 0, 0)),
                  pl.BlockSpec((1, LANES), lambda b, t: (0, b))],
        out_specs=pl.BlockSpec((tc, LANES), lambda b, t: (t, b)),
        out_shape=jax.ShapeDtypeStruct((s, w), BF16),
        scratch_shapes=[pltpu.VMEM((tc, two_half), F32),
                        pltpu.VMEM((2, SUBLANES, half), F32)],
        compiler_params=_params("parallel", "arbitrary"),
        name="s5_scan",
    )(lat, win, pw, cmat, d)


def _s5_tables(lam_re, lam_im, log_dt, b_re, b_im, c_re, c_im, d_skip):
    g, p = lam_re.shape
    hg = b_re.shape[-1]
    gpb = LANES // hg
    nblk = g // gpb
    lr = lam_re.astype(F32)
    li = lam_im.astype(F32)
    dt = jnp.exp(log_dt.astype(F32))[:, None]
    mag = jnp.exp(lr * dt)
    ab_re = mag * jnp.cos(li * dt)
    ab_im = mag * jnp.sin(li * dt)
    den = lr * lr + li * li
    nr = ab_re - 1.0
    ni = ab_im
    coef_re = (nr * lr + ni * li) / den
    coef_im = (ni * lr - nr * li) / den
    br = b_re.astype(F32)
    bi = b_im.astype(F32)
    bb_re = coef_re[..., None] * br - coef_im[..., None] * bi
    bb_im = coef_re[..., None] * bi + coef_im[..., None] * br
    eye = jnp.eye(gpb, dtype=F32)

    def block_diag_in(bb):
        v = jnp.transpose(bb, (0, 2, 1)).reshape(nblk, gpb, hg, p)
        return jnp.einsum("bghp,gk->bghkp", v, eye).reshape(nblk, gpb * hg, gpb * p)

    win = jnp.concatenate([block_diag_in(bb_re), block_diag_in(bb_im)], axis=-1).astype(BF16)

    def block_diag_out(c):
        v = jnp.transpose(c.astype(F32), (0, 2, 1)).reshape(nblk, gpb, p, hg)
        return jnp.einsum("bgph,gk->bgpkh", v, eye).reshape(nblk, gpb * p, gpb * hg)

    cmat = jnp.concatenate([block_diag_out(c_re), -block_diag_out(c_im)], axis=1).astype(BF16)

    pows = [(ab_re, ab_im)]
    for _ in range(SUBLANES - 1):
        qr, qi = pows[-1]
        pows.append((qr * ab_re - qi * ab_im, qr * ab_im + qi * ab_re))
    row = jnp.arange(SUBLANES)[:, None]

    def flat(v):
        return v.reshape(nblk, 1, gpb * p)

    parts = []
    for shift in (1, 2, 4):
        keep = (row >= shift).astype(F32)[None]
        parts.append(flat(pows[shift - 1][0]) * keep)
        parts.append(flat(pows[shift - 1][1]) * keep)
    parts.append(jnp.concatenate([flat(pows[r][0]) for r in range(SUBLANES)], axis=1))
    parts.append(jnp.concatenate([flat(pows[r][1]) for r in range(SUBLANES)], axis=1))
    pw = jnp.concatenate(parts, axis=1)
    return win, pw, cmat, d_skip.astype(F32).reshape(1, -1)


def _glu_kernel(y_ref, w_ref, o_ref):
    y = y_ref[...]
    z = jnp.dot(y, w_ref[...], preferred_element_type=F32)
    o_ref[...] = (y.astype(F32) * _sigmoid(z)).astype(o_ref.dtype)


def glu(y, w, tm=512):
    s, n = y.shape
    return pl.pallas_call(
        _glu_kernel,
        grid=(s // tm,),
        in_specs=[pl.BlockSpec((tm, n), lambda i: (i, 0)),
                  pl.BlockSpec((n, n), lambda i: (0, 0))],
        out_specs=pl.BlockSpec((tm, n), lambda i: (i, 0)),
        out_shape=jax.ShapeDtypeStruct((s, n), BF16),
        compiler_params=_params("parallel"),
        name="s5_glu",
    )(y, w)


def _rms(x, g):
    x = x.astype(F32)
    ms = jnp.mean(x * x, axis=-1, keepdims=True)
    return (x * lax.rsqrt(ms + NORM_EPS) * g).astype(BF16)


def _mla_q_kernel(lat_ref, g_ref, w_ref, wrot_ref, cos_ref, sin_ref, q_ref, *, heads, scale):
    h = _rms(lat_ref[...], g_ref[...])
    main = jnp.dot(h, w_ref[...], preferred_element_type=F32)
    rot = jnp.dot(h, wrot_ref[...], preferred_element_type=F32)
    cos = cos_ref[...]
    sin = sin_ref[...]
    for hd in range(heads):
        base = hd * MLA_QK
        q_ref[:, base:base + MLA_NOPE] = (main[:, base:base + MLA_NOPE] * scale).astype(q_ref.dtype)
        r = main[:, base + MLA_NOPE:base + MLA_QK] * cos + rot[:, hd * LANES:(hd + 1) * LANES] * sin
        q_ref[:, base + MLA_NOPE:base + MLA_QK] = (r * scale).astype(q_ref.dtype)


def mla_q_proj(lat, col0, g_q, w_main, w_rot, cos, sin, heads, tm=512):
    s = lat.shape[0]
    rank = w_main.shape[0]
    scale = (MLA_NOPE + MLA_ROPE) ** -0.5 * math.log2(math.e)
    col_blk = col0 // rank
    return pl.pallas_call(
        functools.partial(_mla_q_kernel, heads=heads, scale=scale),
        grid=(s // tm,),
        in_specs=[pl.BlockSpec((tm, rank), lambda i: (i, col_blk)),
                  pl.BlockSpec((1, rank), lambda i: (0, 0)),
                  pl.BlockSpec(w_main.shape, lambda i: (0, 0)),
                  pl.BlockSpec(w_rot.shape, lambda i: (0, 0)),
                  pl.BlockSpec((tm, LANES), lambda i: (i, 0)),
                  pl.BlockSpec((tm, LANES), lambda i: (i, 0))],
        out_specs=pl.BlockSpec((tm, heads * MLA_QK), lambda i: (i, 0)),
        out_shape=jax.ShapeDtypeStruct((s, heads * MLA_QK), BF16),
        compiler_params=_params("parallel"),
        name="mla_q_proj",
    )(lat, g_q, w_main, w_rot, cos, sin)


def _mla_kv_kernel(lat_ref, g_ref, wk_ref, wv_ref, kr_ref, cos_ref, sin_ref, k_ref, v_ref, *, heads):
    h = _rms(lat_ref[...], g_ref[...])
    kn = jnp.dot(h, wk_ref[...], preferred_element_type=F32)
    v_ref[...] = jnp.dot(h, wv_ref[...], preferred_element_type=F32).astype(v_ref.dtype)
    kr_in = kr_ref[...].astype(F32)
    kr = (kr_in[:, :LANES] * cos_ref[...] + kr_in[:, LANES:] * sin_ref[...]).astype(k_ref.dtype)
    for hd in range(heads):
        base = hd * MLA_QK
        k_ref[:, base:base + MLA_NOPE] = kn[:, hd * MLA_NOPE:(hd + 1) * MLA_NOPE].astype(k_ref.dtype)
        k_ref[:, base + MLA_NOPE:base + MLA_QK] = kr


def mla_kv_proj(lat, col0, g_kv, w_k, w_v, kr_raw, cos, sin, heads, tm=512):
    s = lat.shape[0]
    rank = w_k.shape[0]
    col_blk = col0 // rank
    return pl.pallas_call(
        functools.partial(_mla_kv_kernel, heads=heads),
        grid=(s // tm,),
        in_specs=[pl.BlockSpec((tm, rank), lambda i: (i, col_blk)),
                  pl.BlockSpec((1, rank), lambda i: (0, 0)),
                  pl.BlockSpec(w_k.shape, lambda i: (0, 0)),
                  pl.BlockSpec(w_v.shape, lambda i: (0, 0)),
                  pl.BlockSpec((tm, 2 * LANES), lambda i: (i, 0)),
                  pl.BlockSpec((tm, LANES), lambda i: (i, 0)),
                  pl.BlockSpec((tm, LANES), lambda i: (i, 0))],
        out_specs=[pl.BlockSpec((tm, heads * MLA_QK), lambda i: (i, 0)),
                   pl.BlockSpec((tm, heads * MLA_V), lambda i: (i, 0))],
        out_shape=[jax.ShapeDtypeStruct((s, heads * MLA_QK), BF16),
                   jax.ShapeDtypeStruct((s, heads * MLA_V), BF16)],
        compiler_params=_params("parallel"),
        name="mla_kv_proj",
    )(lat, g_kv, w_k, w_v, kr_raw, cos, sin)


def _mla_attn_kernel(q_ref, k_ref, v_ref, o_ref, *, tq, tk, heads_per_step):
    i = pl.program_id(1)
    q = q_ref[...]
    diag_blocks = tq // tk
    heads = range(heads_per_step)
    qk_cols = [slice(hh * MLA_QK, (hh + 1) * MLA_QK) for hh in heads]
    v_cols = [slice(hh * MLA_V, (hh + 1) * MLA_V) for hh in heads]

    def step(j, carry, masked):
        rows = pl.ds(pl.multiple_of(j * tk, tk), tk)
        k = k_ref[rows, :]
        v = v_ref[rows, :]
        ss = [lax.dot_general(q[:, c], k[:, c], (((1,), (1,)), ((), ())), preferred_element_type=F32)
              for c in qk_cols]
        if masked:
            r = lax.broadcasted_iota(jnp.int32, ss[0].shape, 0) + i * tq
            c = lax.broadcasted_iota(jnp.int32, ss[0].shape, 1) + j * tk
            ss = [jnp.where(c <= r, s, -jnp.inf) for s in ss]
        out = []
        ps = []
        for s, (m, l, acc) in zip(ss, carry):
            m_new = jnp.maximum(m, jnp.max(s, axis=-1, keepdims=True))
            alpha = jnp.exp2(m - m_new)
            p = jnp.exp2(s - m_new)
            ps.append(p.astype(BF16))
            out.append((m_new, alpha * l + jnp.sum(p, axis=-1, keepdims=True), alpha * acc))
        pv = [jnp.dot(p, v[:, c], preferred_element_type=F32) for p, c in zip(ps, v_cols)]
        return tuple((m, l, acc + o) for (m, l, acc), o in zip(out, pv))

    init = tuple((jnp.full((tq, 1), -jnp.inf, F32), jnp.zeros((tq, 1), F32), jnp.zeros((tq, MLA_V), F32))
                 for _ in heads)
    carry = lax.fori_loop(0, i * diag_blocks, lambda j, c: step(j, c, False), init)
    for d in range(diag_blocks):
        carry = step(i * diag_blocks + d, carry, True)
    o_ref[...] = jnp.concatenate([acc / l for _, l, acc in carry], axis=-1).astype(o_ref.dtype)


def mla_attention(q, k, v, heads, tq=1024, tk=1024, heads_per_step=2):
    s = q.shape[0]
    hps = heads_per_step
    return pl.pallas_call(
        functools.partial(_mla_attn_kernel, tq=tq, tk=tk, heads_per_step=hps),
        grid=(heads // hps, s // tq),
        in_specs=[pl.BlockSpec((tq, hps * MLA_QK), lambda h, i: (i, h)),
                  pl.BlockSpec((s, hps * MLA_QK), lambda h, i: (0, h)),
                  pl.BlockSpec((s, hps * MLA_V), lambda h, i: (0, h))],
        out_specs=pl.BlockSpec((tq, hps * MLA_V), lambda h, i: (i, h)),
        out_shape=jax.ShapeDtypeStruct((s, heads * MLA_V), BF16),
        compiler_params=_params("parallel", "arbitrary"),
        name="mla_attention",
    )(q, k, v)


def _sb_attn_kernel(q_ref, k_ref, v_ref, tri_ref, o_ref, acc_ref, rest_ref, *, t, scale, heads_per_step):
    i = pl.program_id(1)
    q = q_ref[...]
    tri = tri_ref[...]
    acc_ref[...] = jnp.zeros_like(acc_ref)
    rest_ref[...] = jnp.zeros_like(rest_ref)
    r_idx = lax.broadcasted_iota(jnp.int32, (t, t), 0)
    c_idx = lax.broadcasted_iota(jnp.int32, (t, t), 1)
    strictly_causal = c_idx < r_idx

    def block(j, diagonal):
        rows = pl.ds(pl.multiple_of(j * t, t), t)
        k = k_ref[rows, :]
        v = v_ref[rows, :]
        acc = acc_ref[...]
        heads = range(heads_per_step)
        cols = [slice(hh * SB_HEAD_DIM, (hh + 1) * SB_HEAD_DIM) for hh in heads]
        zs = [lax.dot_general(q[:, c], k[:, c], (((1,), (1,)), ((), ())), preferred_element_type=F32) * scale
              for c in cols]
        log_betas = [jnp.minimum(z, 0.0) - jnp.log(1.0 + jnp.exp(-jnp.abs(z))) for z in zs]
        log_rests = [lb - z for lb, z in zip(log_betas, zs)]
        if diagonal:
            log_rests = [jnp.where(strictly_causal, lr, 0.0) for lr in log_rests]
        his = [lr.astype(BF16) for lr in log_rests]
        los = [(lr - hi.astype(F32)).astype(BF16) for lr, hi in zip(log_rests, his)]
        laters = [jnp.dot(hi, tri, preferred_element_type=F32) for hi in his]
        laters = [la + jnp.dot(lo, tri, preferred_element_type=F32) for la, lo in zip(laters, los)]
        ws = [jnp.exp(lb + la + rest_ref[hh]) for hh, lb, la in zip(heads, log_betas, laters)]
        if diagonal:
            ws = [jnp.where(strictly_causal, w, 0.0) for w in ws]
        outs = [acc[:, c] + jnp.dot(w.astype(BF16), v[:, c], preferred_element_type=F32)
                for c, w in zip(cols, ws)]
        acc_ref[...] = jnp.concatenate(outs, axis=-1)
        rest_max = None
        for hh, lr in zip(heads, log_rests):
            rest = rest_ref[hh] + jnp.sum(lr, axis=-1, keepdims=True)
            rest_ref[hh] = rest
            head_max = jnp.max(rest)
            rest_max = head_max if rest_max is None else jnp.maximum(rest_max, head_max)
        return rest_max

    def cond(carry):
        j, rest_max = carry
        return jnp.logical_and(j >= 0, rest_max > SB_LOG_WEIGHT_FLOOR)

    def body(carry):
        j, _ = carry
        return j - 1, block(j, False)

    lax.while_loop(cond, body, (i - 1, block(i, True)))
    o_ref[...] = acc_ref[...].astype(o_ref.dtype)


def sb_attention(sbgm, heads, t=256, heads_per_step=4):
    s = sbgm.shape[0]
    width = heads_per_step * SB_HEAD_DIM
    groups = heads // heads_per_step
    tri = (jnp.arange(t)[:, None] > jnp.arange(t)[None, :]).astype(BF16)
    return pl.pallas_call(
        functools.partial(_sb_attn_kernel, t=t, scale=SB_HEAD_DIM ** -0.5, heads_per_step=heads_per_step),
        grid=(groups, s // t),
        in_specs=[pl.BlockSpec((t, width), lambda h, i: (i, h)),
                  pl.BlockSpec((s, width), lambda h, i: (0, groups + h)),
                  pl.BlockSpec((s, width), lambda h, i: (0, 2 * groups + h)),
                  pl.BlockSpec((t, t), lambda h, i: (0, 0))],
        out_specs=pl.BlockSpec((t, width), lambda h, i: (i, h)),
        out_shape=jax.ShapeDtypeStruct((s, heads * SB_HEAD_DIM), BF16),
        scratch_shapes=[pltpu.VMEM((t, width), F32), pltpu.VMEM((heads_per_step, t, 1), F32)],
        compiler_params=_params("parallel", "arbitrary"),
        name="sb_attention",
    )(sbgm, sbgm, sbgm, tri)


def _merge_kernel(y0_ref, y1_ref, y2_ref, g0_ref, g1_ref, g2_ref, m0_ref, m1_ref, m2_ref,
                  w_ref, o_ref, ys_ref):
    j = pl.program_id(1)

    @pl.when(j == 0)
    def _():
        for n, (y_ref, g_ref) in enumerate(((y0_ref, g0_ref), (y1_ref, g1_ref), (y2_ref, g2_ref))):
            g = g_ref[...].astype(F32)
            ys_ref[n] = (y_ref[...].astype(F32) * (g * _sigmoid(g))).astype(BF16)

    acc = None
    for n, m_ref in enumerate((m0_ref, m1_ref, m2_ref)):
        b = jnp.dot(ys_ref[n], w_ref[n], preferred_element_type=F32)
        term = _sigmoid(m_ref[...].astype(F32)) * b
        acc = term if acc is None else acc + term
    o_ref[...] = acc.astype(o_ref.dtype)


def gated_merge(ys, sbgm, w_branch, d_model, tm=512, tn=1024):
    s, w = ys[0].shape
    gate_col0 = 3 * w
    merge_col0 = gate_col0 + N_BRANCH * w
    y_spec = pl.BlockSpec((tm, w), lambda i, j: (i, 0))

    def gate_spec(n):
        return pl.BlockSpec((tm, w), lambda i, j: (i, gate_col0 // w + n))

    def merge_spec(n):
        return pl.BlockSpec((tm, tn), lambda i, j: (i, (merge_col0 + n * d_model) // tn + j))

    return pl.pallas_call(
        _merge_kernel,
        grid=(s // tm, d_model // tn),
        in_specs=[y_spec, y_spec, y_spec,
                  gate_spec(0), gate_spec(1), gate_spec(2),
                  merge_spec(0), merge_spec(1), merge_spec(2),
                  pl.BlockSpec((N_BRANCH, w, tn), lambda i, j: (0, 0, j))],
        out_specs=pl.BlockSpec((tm, tn), lambda i, j: (i, j)),
        out_shape=jax.ShapeDtypeStruct((s, d_model), BF16),
        scratch_shapes=[pltpu.VMEM((N_BRANCH, tm, w), BF16)],
        compiler_params=_params("parallel", "arbitrary"),
        name="gated_merge",
    )(ys[0], ys[1], ys[2], sbgm, sbgm, sbgm, sbgm, sbgm, sbgm, w_branch)


def _ple_kernel(h_ref, wg_ref, p_ref, wp_ref, x_ref, o_ref):
    g = _sigmoid(jnp.dot(h_ref[...], wg_ref[...], preferred_element_type=F32))
    e = jnp.dot(p_ref[...].astype(BF16), wp_ref[...], preferred_element_type=F32)
    o_ref[...] = x_ref[...] + g * e


def ple_gate(h, w_gate, p, w_proj, x, tm=1024, tn=1024):
    s, d = x.shape
    pd = p.shape[1]
    return pl.pallas_call(
        _ple_kernel,
        grid=(s // tm, d // tn),
        in_specs=[pl.BlockSpec((tm, d), lambda i, j: (i, 0)),
                  pl.BlockSpec((d, tn), lambda i, j: (0, j)),
                  pl.BlockSpec((tm, pd), lambda i, j: (i, 0)),
                  pl.BlockSpec((pd, tn), lambda i, j: (0, j)),
                  pl.BlockSpec((tm, tn), lambda i, j: (i, j))],
        out_specs=pl.BlockSpec((tm, tn), lambda i, j: (i, j)),
        out_shape=jax.ShapeDtypeStruct((s, d), F32),
        compiler_params=_params("parallel", "arbitrary"),
        name="ple_gate",
    )(h, w_gate, p, w_proj, x)


def _rot_half_cols(w):
    half = MLA_ROPE // 2
    return jnp.concatenate([-w[..., half:], w[..., :half]], axis=-1)


def _mla_weights(w_uq, w_ukv, heads):
    rank = w_uq.shape[0]
    wq = w_uq.reshape(rank, heads, MLA_NOPE + MLA_ROPE)
    zeros = jnp.zeros((rank, heads, MLA_QK - MLA_NOPE - MLA_ROPE), w_uq.dtype)
    w_main = jnp.concatenate([wq, zeros], axis=-1).reshape(rank, heads * MLA_QK).astype(BF16)
    w_rot = jnp.concatenate([_rot_half_cols(wq[..., MLA_NOPE:]), zeros], axis=-1)
    w_rot = w_rot.reshape(rank, heads * LANES).astype(BF16)
    wkv = w_ukv.reshape(w_ukv.shape[0], heads, MLA_NOPE + MLA_V)
    w_k = wkv[..., :MLA_NOPE].reshape(-1, heads * MLA_NOPE).astype(BF16)
    w_v = wkv[..., MLA_NOPE:].reshape(-1, heads * MLA_V).astype(BF16)
    return w_main, w_rot, w_k, w_v


def _layer(x, p_i, ln_g, w_in, s5, w_glu, g_q, g_kv, w_uq, w_ukv, w_branch, w_out, ple_g, w_ple_gate,
           w_ple_proj, cos, sin):
    s, d = x.shape
    w = d // 2
    q_rank = g_q.shape[0]
    kv_rank = g_kv.shape[0]
    heads = w // MLA_V
    n_lat = w + q_rank + kv_rank
    n_rope_end = n_lat + MLA_ROPE

    w_lat = w_in[:, :n_lat].astype(BF16)
    w_kr = w_in[:, n_lat:n_rope_end]
    zeros = jnp.zeros((d, LANES - MLA_ROPE), w_in.dtype)
    w_kr = jnp.concatenate([w_kr, zeros, _rot_half_cols(w_kr), zeros], axis=-1).astype(BF16)
    w_sbgm = w_in[:, n_rope_end:].astype(BF16)

    h = rmsnorm(x, ln_g, BF16)
    lat = matmul(h, w_lat, BF16, tm=1024, tn=1024, name="proj_lat")
    kr_raw = matmul(h, w_kr, BF16, tm=1024, tn=2 * LANES, name="proj_rope")
    sbgm = matmul(h, w_sbgm, BF16, tm=1024, tn=1024, name="proj_sbgm")

    win, pw, cmat, d_skip = s5
    y_ssm = glu(s5_scan(lat, win, pw, cmat, d_skip), w_glu.astype(BF16))

    w_main, w_rot, w_k, w_v = _mla_weights(w_uq, w_ukv, heads)
    q = mla_q_proj(lat, w, g_q.reshape(1, -1).astype(F32), w_main, w_rot, cos, sin, heads)
    k, v = mla_kv_proj(lat, w + q_rank, g_kv.reshape(1, -1).astype(F32), w_k, w_v, kr_raw, cos, sin, heads)
    y_mla = mla_attention(q, k, v, heads)

    y_sb = sb_attention(sbgm, heads)

    merged = gated_merge((y_ssm, y_mla, y_sb), sbgm, w_branch.astype(BF16), d)
    x = matmul_residual(merged, w_out.astype(BF16), x, tm=1024, tn=1024, name="out_proj")

    hp = rmsnorm(x, ple_g, BF16)
    return ple_gate(hp, w_ple_gate.astype(BF16), p_i, w_ple_proj.astype(BF16), x)


def kernel(x, p, ln_g, w_in, ssm_lam_re, ssm_lam_im, ssm_log_dt, ssm_b_re, ssm_b_im, ssm_c_re, ssm_c_im,
           ssm_d, ssm_w_glu, mla_g_q, mla_g_kv, mla_w_uq, mla_w_ukv, w_branch, w_out, ple_g, w_ple_gate,
           w_ple_proj, final_g):
    bsz, seqlen, d = x.shape
    depth = w_in.shape[0]
    pos = jnp.arange(seqlen, dtype=F32)
    inv_freq = ROPE_THETA ** (-jnp.arange(0, MLA_ROPE, 2, dtype=F32) / MLA_ROPE)
    ang = pos[:, None] * inv_freq[None, :]
    pad = jnp.zeros((seqlen, LANES - MLA_ROPE), F32)
    cos = jnp.concatenate([jnp.cos(ang), jnp.cos(ang), pad], axis=-1)
    sin = jnp.concatenate([jnp.sin(ang), jnp.sin(ang), pad], axis=-1)

    outs = []
    for b in range(bsz):
        xb = x[b]
        for i in range(depth):
            s5 = _s5_tables(ssm_lam_re[i], ssm_lam_im[i], ssm_log_dt[i], ssm_b_re[i], ssm_b_im[i],
                            ssm_c_re[i], ssm_c_im[i], ssm_d[i])
            xb = _layer(xb, p[i, b], ln_g[i], w_in[i], s5, ssm_w_glu[i], mla_g_q[i], mla_g_kv[i],
                        mla_w_uq[i], mla_w_ukv[i], w_branch[i], w_out[i], ple_g[i], w_ple_gate[i],
                        w_ple_proj[i], cos, sin)
        outs.append(rmsnorm(xb, final_g, x.dtype))
    return jnp.stack(outs, axis=0)
```

```python
import functools
import math

import jax
import jax.numpy as jnp
from jax import lax
from jax.experimental import pallas as pl
from jax.experimental.pallas import tpu as pltpu

F32 = jnp.float32
BF16 = jnp.bfloat16

NORM_EPS = 1e-6
ROPE_THETA = 10000.0
DT_GROUP = 16
SSM_STATE = 64
MLA_NOPE = 128
MLA_ROPE = 64
MLA_V = 128
MLA_QK = 256
SB_HEAD_DIM = 128
N_BRANCH = 3

LANES = 128
SUBLANES = 8
VMEM_LIMIT_BYTES = 56 * 1024 * 1024

SB_LOG_WEIGHT_FLOOR = -104.0


def _params(*sem):
    return pltpu.CompilerParams(dimension_semantics=sem, vmem_limit_bytes=VMEM_LIMIT_BYTES)


def _sigmoid(x):
    return 0.5 * (1.0 + jnp.tanh(0.5 * x))


def _rmsnorm_kernel(x_ref, g_ref, o_ref):
    x = x_ref[...].astype(F32)
    ms = jnp.mean(x * x, axis=-1, keepdims=True)
    o_ref[...] = (x * lax.rsqrt(ms + NORM_EPS) * g_ref[...]).astype(o_ref.dtype)


def rmsnorm(x, g, out_dtype, tm=512):
    m, d = x.shape
    return pl.pallas_call(
        _rmsnorm_kernel,
        grid=(m // tm,),
        in_specs=[pl.BlockSpec((tm, d), lambda i: (i, 0)),
                  pl.BlockSpec((1, d), lambda i: (0, 0))],
        out_specs=pl.BlockSpec((tm, d), lambda i: (i, 0)),
        out_shape=jax.ShapeDtypeStruct((m, d), out_dtype),
        compiler_params=_params("parallel"),
        name="rmsnorm",
    )(x, g.reshape(1, d).astype(F32))


def _matmul_kernel(a_ref, w_ref, o_ref):
    o_ref[...] = jnp.dot(a_ref[...], w_ref[...], preferred_element_type=F32).astype(o_ref.dtype)


def matmul(a, w, out_dtype, tm, tn, name):
    m, k = a.shape
    n = w.shape[1]
    return pl.pallas_call(
        _matmul_kernel,
        grid=(m // tm, n // tn),
        in_specs=[pl.BlockSpec((tm, k), lambda i, j: (i, 0)),
                  pl.BlockSpec((k, tn), lambda i, j: (0, j))],
        out_specs=pl.BlockSpec((tm, tn), lambda i, j: (i, j)),
        out_shape=jax.ShapeDtypeStruct((m, n), out_dtype),
        compiler_params=_params("parallel", "arbitrary"),
        name=name,
    )(a, w)


def _matmul_residual_kernel(a_ref, w_ref, r_ref, o_ref):
    o_ref[...] = r_ref[...] + jnp.dot(a_ref[...], w_ref[...], preferred_element_type=F32)


def matmul_residual(a, w, res, tm, tn, name):
    m, k = a.shape
    n = w.shape[1]
    return pl.pallas_call(
        _matmul_residual_kernel,
        grid=(m // tm, n // tn),
        in_specs=[pl.BlockSpec((tm, k), lambda i, j: (i, 0)),
                  pl.BlockSpec((k, tn), lambda i, j: (0, j)),
                  pl.BlockSpec((tm, tn), lambda i, j: (i, j))],
        out_specs=pl.BlockSpec((tm, tn), lambda i, j: (i, j)),
        out_shape=jax.ShapeDtypeStruct((m, n), F32),
        compiler_params=_params("parallel", "arbitrary"),
        name=name,
    )(a, w, res)


def _repack_kernel(wm_ref, wn_ref, o_ref, *, first_shifted, first_halved):
    j = pl.program_id(0)

    @pl.when(j < first_shifted)
    def _():
        o_ref[...] = wm_ref[...].astype(o_ref.dtype)

    @pl.when(j >= first_shifted)
    def _():
        gain = jnp.where(j >= first_halved, 0.5, 1.0).astype(F32)
        n_chunks = wm_ref.shape[1] // LANES
        keep = LANES - MLA_ROPE
        lane = lax.broadcasted_iota(jnp.int32, (wm_ref.shape[0], LANES), 1)
        rolled = pltpu.roll(wm_ref[:, :LANES], keep, 1)
        for c in range(n_chunks):
            nxt = wm_ref[:, (c + 1) * LANES:(c + 2) * LANES] if c + 1 < n_chunks else wn_ref[...]
            rolled_next = pltpu.roll(nxt, keep, 1)
            chunk = jnp.where(lane < keep, rolled, rolled_next)
            o_ref[:, c * LANES:(c + 1) * LANES] = (chunk * gain).astype(o_ref.dtype)
            rolled = rolled_next


def repack_w_in(w_in, layer, n_lat, n_halved_from, tr=512, tn=1024):
    _, k, n_in = w_in.shape
    n_out = n_in - MLA_ROPE - (n_in - MLA_ROPE) % tn
    assert n_out == n_in - MLA_ROPE and n_lat % tn == 0 and n_halved_from % tn == 0
    per = tn // LANES
    return pl.pallas_call(
        functools.partial(_repack_kernel, first_shifted=n_lat // tn, first_halved=n_halved_from // tn),
        grid=(n_out // tn, k // tr),
        in_specs=[pl.BlockSpec((None, tr, tn), lambda j, r: (layer, r, j)),
                  pl.BlockSpec((None, tr, LANES), lambda j, r: (layer, r, per * (j + 1)))],
        out_specs=pl.BlockSpec((tr, tn), lambda j, r: (r, j)),
        out_shape=jax.ShapeDtypeStruct((k, n_out), BF16),
        compiler_params=_params("parallel", "parallel"),
        name="repack_w_in",
    )(w_in, w_in)


def _rope_proj_kernel(a_ref, w_ref, o_ref):
    o_ref[...] = jnp.dot(a_ref[...], w_ref[...].astype(BF16), preferred_element_type=F32).astype(o_ref.dtype)


def rope_proj(h, w_in, layer, col0, tm=1024):
    m, k = h.shape
    return pl.pallas_call(
        _rope_proj_kernel,
        grid=(m // tm,),
        in_specs=[pl.BlockSpec((tm, k), lambda i: (i, 0)),
                  pl.BlockSpec((None, k, LANES), lambda i: (layer, 0, col0 // LANES))],
        out_specs=pl.BlockSpec((tm, LANES), lambda i: (i, 0)),
        out_shape=jax.ShapeDtypeStruct((m, LANES), BF16),
        compiler_params=_params("parallel"),
        name="proj_rope",
    )(h, w_in)


def _s5_scan_kernel(u_ref, win_ref, pw_ref, cmat_ref, d_ref, y_ref, h_ref, carry_ref, *, n_tiles, half):
    t = pl.program_id(1)

    @pl.when(t == 0)
    def _():
        carry_ref[...] = jnp.zeros_like(carry_ref)

    u = u_ref[...]
    h_ref[...] = jnp.dot(u, win_ref[0], preferred_element_type=F32)

    def tile(i, carry):
        c_re, c_im = carry
        rows = pl.ds(pl.multiple_of(i * SUBLANES, SUBLANES), SUBLANES)
        x_re = h_ref[rows, :half]
        x_im = h_ref[rows, half:]
        for lvl, shift in enumerate((1, 2, 4)):
            a_re = pw_ref[0, (2 * lvl) * SUBLANES:(2 * lvl + 1) * SUBLANES, :]
            a_im = pw_ref[0, (2 * lvl + 1) * SUBLANES:(2 * lvl + 2) * SUBLANES, :]
            s_re = pltpu.roll(x_re, shift, 0)
            s_im = pltpu.roll(x_im, shift, 0)
            x_re, x_im = (x_re + (a_re * s_re - a_im * s_im),
                          x_im + (a_re * s_im + a_im * s_re))
        p_re = pw_ref[0, 6 * SUBLANES:7 * SUBLANES, :]
        p_im = pw_ref[0, 7 * SUBLANES:8 * SUBLANES, :]
        x_re, x_im = (x_re + (p_re * c_re - p_im * c_im),
                      x_im + (p_re * c_im + p_im * c_re))
        h_ref[rows, :half] = x_re
        h_ref[rows, half:] = x_im
        last = SUBLANES - 1
        return (jnp.broadcast_to(x_re[last:, :], x_re.shape),
                jnp.broadcast_to(x_im[last:, :], x_im.shape))

    c_re, c_im = lax.fori_loop(0, n_tiles, tile, (carry_ref[0], carry_ref[1]), unroll=4)
    carry_ref[0] = c_re
    carry_ref[1] = c_im

    y = jnp.dot(h_ref[...].astype(BF16), cmat_ref[0], preferred_element_type=F32)
    y = y + d_ref[...] * u.astype(F32)
    c0 = math.sqrt(2.0 / math.pi)
    y = 0.5 * y * (1.0 + jnp.tanh(c0 * (y + 0.044715 * (y * y * y))))
    y_ref[...] = y.astype(y_ref.dtype)


def s5_scan(lat, win, pw, cmat, d, tc=512):
    s = lat.shape[0]
    nblk, _, two_half = win.shape
    half = two_half // 2
    w = nblk * LANES
    kern = functools.partial(_s5_scan_kernel, n_tiles=tc // SUBLANES, half=half)
    return pl.pallas_call(
        kern,
        grid=(nblk, s // tc),
        in_specs=[pl.BlockSpec((tc, LANES), lambda b, t: (t, b)),
                  pl.BlockSpec((1, LANES, two_half), lambda b, t: (b, 0, 0)),
                  pl.BlockSpec((1, 8 * SUBLANES, half), lambda b, t: (b, 0, 0)),
                  pl.BlockSpec((1, two_half, LANES), lambda b, t: (b, 0, 0)),
                  pl.BlockSpec((1, LANES), lambda b, t: (0, b))],
        out_specs=pl.BlockSpec((tc, LANES), lambda b, t: (t, b)),
        out_shape=jax.ShapeDtypeStruct((s, w), BF16),
        scratch_shapes=[pltpu.VMEM((tc, two_half), F32),
                        pltpu.VMEM((2, SUBLANES, half), F32)],
        compiler_params=_params("parallel", "arbitrary"),
        name="s5_scan",
    )(lat, win, pw, cmat, d)


def _s5_tables(lam_re, lam_im, log_dt, b_re, b_im, c_re, c_im, d_skip):
    g, p = lam_re.shape
    hg = b_re.shape[-1]
    gpb = LANES // hg
    nblk = g // gpb
    lr = lam_re.astype(F32)
    li = lam_im.astype(F32)
    dt = jnp.exp(log_dt.astype(F32))[:, None]
    mag = jnp.exp(lr * dt)
    ab_re = mag * jnp.cos(li * dt)
    ab_im = mag * jnp.sin(li * dt)
    den = lr * lr + li * li
    nr = ab_re - 1.0
    ni = ab_im
    coef_re = (nr * lr + ni * li) / den
    coef_im = (ni * lr - nr * li) / den
    br = b_re.astype(F32)
    bi = b_im.astype(F32)
    bb_re = coef_re[..., None] * br - coef_im[..., None] * bi
    bb_im = coef_re[..., None] * bi + coef_im[..., None] * br
    eye = jnp.eye(gpb, dtype=F32)

    def block_diag_in(bb):
        v = jnp.transpose(bb, (0, 2, 1)).reshape(nblk, gpb, hg, p)
        return jnp.einsum("bghp,gk->bghkp", v, eye).reshape(nblk, gpb * hg, gpb * p)

    win = jnp.concatenate([block_diag_in(bb_re), block_diag_in(bb_im)], axis=-1).astype(BF16)

    def block_diag_out(c):
        v = jnp.transpose(c.astype(F32), (0, 2, 1)).reshape(nblk, gpb, p, hg)
        return jnp.einsum("bgph,gk->bgpkh", v, eye).reshape(nblk, gpb * p, gpb * hg)

    cmat = jnp.concatenate([block_diag_out(c_re), -block_diag_out(c_im)], axis=1).astype(BF16)

    pows = [(ab_re, ab_im)]
    for _ in range(SUBLANES - 1):
        qr, qi = pows[-1]
        pows.append((qr * ab_re - qi * ab_im, qr * ab_im + qi * ab_re))
    row = jnp.arange(SUBLANES)[:, None]

    def flat(v):
        return v.reshape(nblk, 1, gpb * p)

    parts = []
    for shift in (1, 2, 4):
        keep = (row >= shift).astype(F32)[None]
        parts.append(flat(pows[shift - 1][0]) * keep)
        parts.append(flat(pows[shift - 1][1]) * keep)
    parts.append(jnp.concatenate([flat(pows[r][0]) for r in range(SUBLANES)], axis=1))
    parts.append(jnp.concatenate([flat(pows[r][1]) for r in range(SUBLANES)], axis=1))
    pw = jnp.concatenate(parts, axis=1)
    return win, pw, cmat, d_skip.astype(F32).reshape(1, -1)


def _glu_kernel(y_ref, w_ref, o_ref):
    y = y_ref[...]
    z = jnp.dot(y, w_ref[...], preferred_element_type=F32)
    o_ref[...] = (y.astype(F32) * _sigmoid(z)).astype(o_ref.dtype)


def glu(y, w, tm=512):
    s, n = y.shape
    return pl.pallas_call(
        _glu_kernel,
        grid=(s // tm,),
        in_specs=[pl.BlockSpec((tm, n), lambda i: (i, 0)),
                  pl.BlockSpec((n, n), lambda i: (0, 0))],
        out_specs=pl.BlockSpec((tm, n), lambda i: (i, 0)),
        out_shape=jax.ShapeDtypeStruct((s, n), BF16),
        compiler_params=_params("parallel"),
        name="s5_glu",
    )(y, w)


def _rms(x, g):
    x = x.astype(F32)
    ms = jnp.mean(x * x, axis=-1, keepdims=True)
    return (x * lax.rsqrt(ms + NORM_EPS) * g).astype(BF16)


def _mla_q_kernel(lat_ref, g_ref, w_ref, wrot_ref, cos_ref, sin_ref, q_ref, *, heads, scale):
    h = _rms(lat_ref[...], g_ref[...])
    main = jnp.dot(h, w_ref[...], preferred_element_type=F32)
    rot = jnp.dot(h, wrot_ref[...], preferred_element_type=F32)
    cos = cos_ref[...]
    sin = sin_ref[...]
    for hd in range(heads):
        base = hd * MLA_QK
        q_ref[:, base:base + MLA_NOPE] = (main[:, base:base + MLA_NOPE] * scale).astype(q_ref.dtype)
        r = main[:, base + MLA_NOPE:base + MLA_QK] * cos + rot[:, hd * LANES:(hd + 1) * LANES] * sin
        q_ref[:, base + MLA_NOPE:base + MLA_QK] = (r * scale).astype(q_ref.dtype)


def mla_q_proj(lat, col0, g_q, w_main, w_rot, cos, sin, heads, tm=512):
    s = lat.shape[0]
    rank = w_main.shape[0]
    scale = (MLA_NOPE + MLA_ROPE) ** -0.5 * math.log2(math.e)
    col_blk = col0 // rank
    return pl.pallas_call(
        functools.partial(_mla_q_kernel, heads=heads, scale=scale),
        grid=(s // tm,),
        in_specs=[pl.BlockSpec((tm, rank), lambda i: (i, col_blk)),
                  pl.BlockSpec((1, rank), lambda i: (0, 0)),
                  pl.BlockSpec(w_main.shape, lambda i: (0, 0)),
                  pl.BlockSpec(w_rot.shape, lambda i: (0, 0)),
                  pl.BlockSpec((tm, LANES), lambda i: (i, 0)),
                  pl.BlockSpec((tm, LANES), lambda i: (i, 0))],
        out_specs=pl.BlockSpec((tm, heads * MLA_QK), lambda i: (i, 0)),
        out_shape=jax.ShapeDtypeStruct((s, heads * MLA_QK), BF16),
        compiler_params=_params("parallel"),
        name="mla_q_proj",
    )(lat, g_q, w_main, w_rot, cos, sin)


def _mla_kv_kernel(lat_ref, g_ref, wk_ref, wv_ref, kr_ref, cos_ref, sin_ref, k_ref, v_ref, *, heads):
    h = _rms(lat_ref[...], g_ref[...])
    kn = jnp.dot(h, wk_ref[...], preferred_element_type=F32)
    v_ref[...] = jnp.dot(h, wv_ref[...], preferred_element_type=F32).astype(v_ref.dtype)
    x = kr_ref[...].astype(F32)
    half = MLA_ROPE // 2
    lane = lax.broadcasted_iota(jnp.int32, x.shape, 1)
    rot = jnp.where(lane < half, -pltpu.roll(x, LANES - half, 1), pltpu.roll(x, half, 1))
    kr = (x * cos_ref[...] + rot * sin_ref[...]).astype(k_ref.dtype)
    for hd in range(heads):
        base = hd * MLA_QK
        k_ref[:, base:base + MLA_NOPE] = kn[:, hd * MLA_NOPE:(hd + 1) * MLA_NOPE].astype(k_ref.dtype)
        k_ref[:, base + MLA_NOPE:base + MLA_QK] = kr


def mla_kv_proj(lat, col0, g_kv, w_k, w_v, kr_raw, cos, sin, heads, tm=512):
    s = lat.shape[0]
    rank = w_k.shape[0]
    col_blk = col0 // rank
    return pl.pallas_call(
        functools.partial(_mla_kv_kernel, heads=heads),
        grid=(s // tm,),
        in_specs=[pl.BlockSpec((tm, rank), lambda i: (i, col_blk)),
                  pl.BlockSpec((1, rank), lambda i: (0, 0)),
                  pl.BlockSpec(w_k.shape, lambda i: (0, 0)),
                  pl.BlockSpec(w_v.shape, lambda i: (0, 0)),
                  pl.BlockSpec((tm, LANES), lambda i: (i, 0)),
                  pl.BlockSpec((tm, LANES), lambda i: (i, 0)),
                  pl.BlockSpec((tm, LANES), lambda i: (i, 0))],
        out_specs=[pl.BlockSpec((tm, heads * MLA_QK), lambda i: (i, 0)),
                   pl.BlockSpec((tm, heads * MLA_V), lambda i: (i, 0))],
        out_shape=[jax.ShapeDtypeStruct((s, heads * MLA_QK), BF16),
                   jax.ShapeDtypeStruct((s, heads * MLA_V), BF16)],
        compiler_params=_params("parallel"),
        name="mla_kv_proj",
    )(lat, g_kv, w_k, w_v, kr_raw, cos, sin)


def _mla_attn_kernel(q_ref, k_ref, v_ref, o_ref, *, tq, tk, heads_per_step):
    i = pl.program_id(1)
    q = q_ref[...]
    diag_blocks = tq // tk
    heads = range(heads_per_step)
    qk_cols = [slice(hh * MLA_QK, (hh + 1) * MLA_QK) for hh in heads]
    v_cols = [slice(hh * MLA_V, (hh + 1) * MLA_V) for hh in heads]

    def step(j, carry, masked):
        rows = pl.ds(pl.multiple_of(j * tk, tk), tk)
        k = k_ref[rows, :]
        v = v_ref[rows, :]
        ss = [lax.dot_general(q[:, c], k[:, c], (((1,), (1,)), ((), ())), preferred_element_type=F32)
              for c in qk_cols]
        if masked:
            r = lax.broadcasted_iota(jnp.int32, ss[0].shape, 0) + i * tq
            c = lax.broadcasted_iota(jnp.int32, ss[0].shape, 1) + j * tk
            ss = [jnp.where(c <= r, s, -jnp.inf) for s in ss]
        out = []
        ps = []
        for s, (m, l, acc) in zip(ss, carry):
            m_new = jnp.maximum(m, jnp.max(s, axis=-1, keepdims=True))
            alpha = jnp.exp2(m - m_new)
            p = jnp.exp2(s - m_new)
            ps.append(p.astype(BF16))
            out.append((m_new, alpha * l + jnp.sum(p, axis=-1, keepdims=True), alpha * acc))
        pv = [jnp.dot(p, v[:, c], preferred_element_type=F32) for p, c in zip(ps, v_cols)]
        return tuple((m, l, acc + o) for (m, l, acc), o in zip(out, pv))

    init = tuple((jnp.full((tq, 1), -jnp.inf, F32), jnp.zeros((tq, 1), F32), jnp.zeros((tq, MLA_V), F32))
                 for _ in heads)
    carry = lax.fori_loop(0, i * diag_blocks, lambda j, c: step(j, c, False), init)
    for d in range(diag_blocks):
        carry = step(i * diag_blocks + d, carry, True)
    o_ref[...] = jnp.concatenate([acc / l for _, l, acc in carry], axis=-1).astype(o_ref.dtype)


def mla_attention(q, k, v, heads, tq=1024, tk=1024, heads_per_step=2):
    s = q.shape[0]
    hps = heads_per_step
    return pl.pallas_call(
        functools.partial(_mla_attn_kernel, tq=tq, tk=tk, heads_per_step=hps),
        grid=(heads // hps, s // tq),
        in_specs=[pl.BlockSpec((tq, hps * MLA_QK), lambda h, i: (i, h)),
                  pl.BlockSpec((s, hps * MLA_QK), lambda h, i: (0, h)),
                  pl.BlockSpec((s, hps * MLA_V), lambda h, i: (0, h))],
        out_specs=pl.BlockSpec((tq, hps * MLA_V), lambda h, i: (i, h)),
        out_shape=jax.ShapeDtypeStruct((s, heads * MLA_V), BF16),
        compiler_params=_params("parallel", "arbitrary"),
        name="mla_attention",
    )(q, k, v)


def _sb_attn_kernel(q_ref, k_ref, v_ref, tri_ref, o_ref, acc_ref, rest_ref, *, t, scale, heads_per_step):
    i = pl.program_id(1)
    q = q_ref[...]
    tri = tri_ref[...]
    acc_ref[...] = jnp.zeros_like(acc_ref)
    rest_ref[...] = jnp.zeros_like(rest_ref)
    r_idx = lax.broadcasted_iota(jnp.int32, (t, t), 0)
    c_idx = lax.broadcasted_iota(jnp.int32, (t, t), 1)
    strictly_causal = c_idx < r_idx

    def block(j, diagonal):
        rows = pl.ds(pl.multiple_of(j * t, t), t)
        k = k_ref[rows, :]
        v = v_ref[rows, :]
        acc = acc_ref[...]
        heads = range(heads_per_step)
        cols = [slice(hh * SB_HEAD_DIM, (hh + 1) * SB_HEAD_DIM) for hh in heads]
        zs = [lax.dot_general(q[:, c], k[:, c], (((1,), (1,)), ((), ())), preferred_element_type=F32) * scale
              for c in cols]
        log_betas = [jnp.minimum(z, 0.0) - jnp.log(1.0 + jnp.exp(-jnp.abs(z))) for z in zs]
        log_rests = [lb - z for lb, z in zip(log_betas, zs)]
        if diagonal:
            log_rests = [jnp.where(strictly_causal, lr, 0.0) for lr in log_rests]
        his = [lr.astype(BF16) for lr in log_rests]
        los = [(lr - hi.astype(F32)).astype(BF16) for lr, hi in zip(log_rests, his)]
        laters = [jnp.dot(hi, tri, preferred_element_type=F32) for hi in his]
        laters = [la + jnp.dot(lo, tri, preferred_element_type=F32) for la, lo in zip(laters, los)]
        ws = [jnp.exp(lb + la + rest_ref[hh]) for hh, lb, la in zip(heads, log_betas, laters)]
        if diagonal:
            ws = [jnp.where(strictly_causal, w, 0.0) for w in ws]
        outs = [acc[:, c] + jnp.dot(w.astype(BF16), v[:, c], preferred_element_type=F32)
                for c, w in zip(cols, ws)]
        acc_ref[...] = jnp.concatenate(outs, axis=-1)
        rest_max = None
        for hh, lr in zip(heads, log_rests):
            rest = rest_ref[hh] + jnp.sum(lr, axis=-1, keepdims=True)
            rest_ref[hh] = rest
            head_max = jnp.max(rest)
            rest_max = head_max if rest_max is None else jnp.maximum(rest_max, head_max)
        return rest_max

    def cond(carry):
        j, rest_max = carry
        return jnp.logical_and(j >= 0, rest_max > SB_LOG_WEIGHT_FLOOR)

    def body(carry):
        j, _ = carry
        return j - 1, block(j, False)

    lax.while_loop(cond, body, (i - 1, block(i, True)))
    o_ref[...] = acc_ref[...].astype(o_ref.dtype)


def sb_attention(proj, col0, heads, t=256, heads_per_step=4):
    s = proj.shape[0]
    width = heads_per_step * SB_HEAD_DIM
    groups = heads // heads_per_step
    base = col0 // width
    tri = (jnp.arange(t)[:, None] > jnp.arange(t)[None, :]).astype(BF16)
    return pl.pallas_call(
        functools.partial(_sb_attn_kernel, t=t, scale=SB_HEAD_DIM ** -0.5, heads_per_step=heads_per_step),
        grid=(groups, s // t),
        in_specs=[pl.BlockSpec((t, width), lambda h, i: (i, base + h)),
                  pl.BlockSpec((s, width), lambda h, i: (0, base + groups + h)),
                  pl.BlockSpec((s, width), lambda h, i: (0, base + 2 * groups + h)),
                  pl.BlockSpec((t, t), lambda h, i: (0, 0))],
        out_specs=pl.BlockSpec((t, width), lambda h, i: (i, h)),
        out_shape=jax.ShapeDtypeStruct((s, heads * SB_HEAD_DIM), BF16),
        scratch_shapes=[pltpu.VMEM((t, width), F32), pltpu.VMEM((heads_per_step, t, 1), F32)],
        compiler_params=_params("parallel", "arbitrary"),
        name="sb_attention",
    )(proj, proj, proj, tri)


def _merge_kernel(y0_ref, y1_ref, y2_ref, g0_ref, g1_ref, g2_ref, m0_ref, m1_ref, m2_ref,
                  w_ref, o_ref, ys_ref):
    j = pl.program_id(1)

    @pl.when(j == 0)
    def _():
        for n, (y_ref, g_ref) in enumerate(((y0_ref, g0_ref), (y1_ref, g1_ref), (y2_ref, g2_ref))):
            g = g_ref[...].astype(F32)
            ys_ref[n] = (y_ref[...].astype(F32) * (g * _sigmoid(g))).astype(BF16)

    acc = None
    for n, m_ref in enumerate((m0_ref, m1_ref, m2_ref)):
        b = jnp.dot(ys_ref[n], w_ref[n], preferred_element_type=F32)
        term = b + b * jnp.tanh(m_ref[...].astype(F32))
        acc = term if acc is None else acc + term
    o_ref[...] = acc.astype(o_ref.dtype)


def gated_merge(ys, proj, col0, half_w_branch, d_model, tm=512, tn=1024):
    s, w = ys[0].shape
    gate_col0 = col0 + 3 * w
    merge_col0 = gate_col0 + N_BRANCH * w
    y_spec = pl.BlockSpec((tm, w), lambda i, j: (i, 0))

    def gate_spec(n):
        return pl.BlockSpec((tm, w), lambda i, j: (i, gate_col0 // w + n))

    def merge_spec(n):
        return pl.BlockSpec((tm, tn), lambda i, j: (i, (merge_col0 + n * d_model) // tn + j))

    return pl.pallas_call(
        _merge_kernel,
        grid=(s // tm, d_model // tn),
        in_specs=[y_spec, y_spec, y_spec,
                  gate_spec(0), gate_spec(1), gate_spec(2),
                  merge_spec(0), merge_spec(1), merge_spec(2),
                  pl.BlockSpec((N_BRANCH, w, tn), lambda i, j: (0, 0, j))],
        out_specs=pl.BlockSpec((tm, tn), lambda i, j: (i, j)),
        out_shape=jax.ShapeDtypeStruct((s, d_model), BF16),
        scratch_shapes=[pltpu.VMEM((N_BRANCH, tm, w), BF16)],
        compiler_params=_params("parallel", "arbitrary"),
        name="gated_merge",
    )(ys[0], ys[1], ys[2], proj, proj, proj, proj, proj, proj, half_w_branch)


def _ple_kernel(h_ref, wg_ref, p_ref, wp_ref, x_ref, o_ref):
    g = _sigmoid(jnp.dot(h_ref[...], wg_ref[...], preferred_element_type=F32))
    e = jnp.dot(p_ref[...].astype(BF16), wp_ref[...], preferred_element_type=F32)
    o_ref[...] = x_ref[...] + g * e


def ple_gate(h, w_gate, p, w_proj, x, tm=1024, tn=1024):
    s, d = x.shape
    pd = p.shape[1]
    return pl.pallas_call(
        _ple_kernel,
        grid=(s // tm, d // tn),
        in_specs=[pl.BlockSpec((tm, d), lambda i, j: (i, 0)),
                  pl.BlockSpec((d, tn), lambda i, j: (0, j)),
                  pl.BlockSpec((tm, pd), lambda i, j: (i, 0)),
                  pl.BlockSpec((pd, tn), lambda i, j: (0, j)),
                  pl.BlockSpec((tm, tn), lambda i, j: (i, j))],
        out_specs=pl.BlockSpec((tm, tn), lambda i, j: (i, j)),
        out_shape=jax.ShapeDtypeStruct((s, d), F32),
        compiler_params=_params("parallel", "arbitrary"),
        name="ple_gate",
    )(h, w_gate, p, w_proj, x)


def _rot_half_cols(w):
    half = MLA_ROPE // 2
    return jnp.concatenate([-w[..., half:], w[..., :half]], axis=-1)


def _mla_weights(w_uq, w_ukv, heads):
    rank = w_uq.shape[0]
    wq = w_uq.reshape(rank, heads, MLA_NOPE + MLA_ROPE)
    zeros = jnp.zeros((rank, heads, MLA_QK - MLA_NOPE - MLA_ROPE), w_uq.dtype)
    w_main = jnp.concatenate([wq, zeros], axis=-1).reshape(rank, heads * MLA_QK).astype(BF16)
    w_rot = jnp.concatenate([_rot_half_cols(wq[..., MLA_NOPE:]), zeros], axis=-1)
    w_rot = w_rot.reshape(rank, heads * LANES).astype(BF16)
    wkv = w_ukv.reshape(w_ukv.shape[0], heads, MLA_NOPE + MLA_V)
    w_k = wkv[..., :MLA_NOPE].reshape(-1, heads * MLA_NOPE).astype(BF16)
    w_v = wkv[..., MLA_NOPE:].reshape(-1, heads * MLA_V).astype(BF16)
    return w_main, w_rot, w_k, w_v


def _layer(x, p_i, ln_g, w_in, layer, s5, w_glu, g_q, g_kv, w_uq, w_ukv, w_branch, w_out, ple_g, w_ple_gate,
           w_ple_proj, cos, sin):
    s, d = x.shape
    w = d // 2
    q_rank = g_q.shape[0]
    kv_rank = g_kv.shape[0]
    heads = w // MLA_V
    n_lat = w + q_rank + kv_rank

    w_packed = repack_w_in(w_in, layer, n_lat, n_lat + 3 * w + N_BRANCH * w)

    h = rmsnorm(x, ln_g, BF16)
    proj = matmul(h, w_packed, BF16, tm=1024, tn=1024, name="proj_in")
    kr_raw = rope_proj(h, w_in, layer, n_lat)

    win, pw, cmat, d_skip = s5
    y_ssm = glu(s5_scan(proj, win, pw, cmat, d_skip), w_glu.astype(BF16))

    w_main, w_rot, w_k, w_v = _mla_weights(w_uq, w_ukv, heads)
    q = mla_q_proj(proj, w, g_q.reshape(1, -1).astype(F32), w_main, w_rot, cos, sin, heads)
    k, v = mla_kv_proj(proj, w + q_rank, g_kv.reshape(1, -1).astype(F32), w_k, w_v, kr_raw, cos, sin, heads)
    y_mla = mla_attention(q, k, v, heads)

    y_sb = sb_attention(proj, n_lat, heads)

    merged = gated_merge((y_ssm, y_mla, y_sb), proj, n_lat, (0.5 * w_branch).astype(BF16), d)
    x = matmul_residual(merged, w_out.astype(BF16), x, tm=1024, tn=1024, name="out_proj")

    hp = rmsnorm(x, ple_g, BF16)
    return ple_gate(hp, w_ple_gate.astype(BF16), p_i, w_ple_proj.astype(BF16), x)


def kernel(x, p, ln_g, w_in, ssm_lam_re, ssm_lam_im, ssm_log_dt, ssm_b_re, ssm_b_im, ssm_c_re, ssm_c_im,
           ssm_d, ssm_w_glu, mla_g_q, mla_g_kv, mla_w_uq, mla_w_ukv, w_branch, w_out, ple_g, w_ple_gate,
           w_ple_proj, final_g):
    bsz, seqlen, d = x.shape
    depth = w_in.shape[0]
    pos = jnp.arange(seqlen, dtype=F32)
    inv_freq = ROPE_THETA ** (-jnp.arange(0, MLA_ROPE, 2, dtype=F32) / MLA_ROPE)
    ang = pos[:, None] * inv_freq[None, :]
    pad = jnp.zeros((seqlen, LANES - MLA_ROPE), F32)
    cos = jnp.concatenate([jnp.cos(ang), jnp.cos(ang), pad], axis=-1)
    sin = jnp.concatenate([jnp.sin(ang), jnp.sin(ang), pad], axis=-1)

    outs = []
    for b in range(bsz):
        xb = x[b]
        for i in range(depth):
            s5 = _s5_tables(ssm_lam_re[i], ssm_lam_im[i], ssm_log_dt[i], ssm_b_re[i], ssm_b_im[i],
                            ssm_c_re[i], ssm_c_im[i], ssm_d[i])
            xb = _layer(xb, p[i, b], ln_g[i], w_in, i, s5, ssm_w_glu[i], mla_g_q[i], mla_g_kv[i],
                        mla_w_uq[i], mla_w_ukv[i], w_branch[i], w_out[i], ple_g[i], w_ple_gate[i],
                        w_ple_proj[i], cos, sin)
        outs.append(rmsnorm(xb, final_g, x.dtype))
    return jnp.stack(outs, axis=0)
```

```python
import functools
import math

import jax
import jax.numpy as jnp
from jax import lax
from jax.experimental import pallas as pl
from jax.experimental.pallas import tpu as pltpu

F32 = jnp.float32
BF16 = jnp.bfloat16

NORM_EPS = 1e-6
ROPE_THETA = 10000.0
DT_GROUP = 16
SSM_STATE = 64
MLA_NOPE = 128
MLA_ROPE = 64
MLA_V = 128
MLA_QK = 256
SB_HEAD_DIM = 128
N_BRANCH = 3

LANES = 128
SUBLANES = 8
VMEM_LIMIT_BYTES = 56 * 1024 * 1024

SB_LOG_WEIGHT_FLOOR = -104.0


def _params(*sem):
    return pltpu.CompilerParams(dimension_semantics=sem, vmem_limit_bytes=VMEM_LIMIT_BYTES)


def _sigmoid(x):
    return 0.5 * (1.0 + jnp.tanh(0.5 * x))


def _rmsnorm_kernel(x_ref, g_ref, o_ref):
    x = x_ref[...].astype(F32)
    ms = jnp.mean(x * x, axis=-1, keepdims=True)
    o_ref[...] = (x * lax.rsqrt(ms + NORM_EPS) * g_ref[...]).astype(o_ref.dtype)


def rmsnorm(x, g, out_dtype, tm=512):
    m, d = x.shape
    return pl.pallas_call(
        _rmsnorm_kernel,
        grid=(m // tm,),
        in_specs=[pl.BlockSpec((tm, d), lambda i: (i, 0)),
                  pl.BlockSpec((1, d), lambda i: (0, 0))],
        out_specs=pl.BlockSpec((tm, d), lambda i: (i, 0)),
        out_shape=jax.ShapeDtypeStruct((m, d), out_dtype),
        compiler_params=_params("parallel"),
        name="rmsnorm",
    )(x, g.reshape(1, d).astype(F32))


def _matmul_kernel(a_ref, w_ref, o_ref):
    o_ref[...] = jnp.dot(a_ref[...], w_ref[...], preferred_element_type=F32).astype(o_ref.dtype)


def matmul(a, w, out_dtype, tm, tn, name):
    m, k = a.shape
    n = w.shape[1]
    return pl.pallas_call(
        _matmul_kernel,
        grid=(m // tm, n // tn),
        in_specs=[pl.BlockSpec((tm, k), lambda i, j: (i, 0)),
                  pl.BlockSpec((k, tn), lambda i, j: (0, j))],
        out_specs=pl.BlockSpec((tm, tn), lambda i, j: (i, j)),
        out_shape=jax.ShapeDtypeStruct((m, n), out_dtype),
        compiler_params=_params("parallel", "arbitrary"),
        name=name,
    )(a, w)


def _store_row_and_norm(x, j, row_ref, g_ref, n_ref, tn):
    row_ref[:, pl.ds(pl.multiple_of(j * tn, tn), tn)] = x

    @pl.when(j == pl.num_programs(1) - 1)
    def _():
        row = row_ref[...]
        ms = jnp.mean(row * row, axis=-1, keepdims=True)
        n_ref[...] = (row * lax.rsqrt(ms + NORM_EPS) * g_ref[...]).astype(n_ref.dtype)


def _matmul_residual_norm_kernel(a_ref, w_ref, r_ref, g_ref, x_ref, n_ref, row_ref, *, tn):
    x = r_ref[...] + jnp.dot(a_ref[...], w_ref[...], preferred_element_type=F32)
    x_ref[...] = x
    _store_row_and_norm(x, pl.program_id(1), row_ref, g_ref, n_ref, tn)


def matmul_residual_norm(a, w, res, g, tm, tn, name):
    m, k = a.shape
    n = w.shape[1]
    return pl.pallas_call(
        functools.partial(_matmul_residual_norm_kernel, tn=tn),
        grid=(m // tm, n // tn),
        in_specs=[pl.BlockSpec((tm, k), lambda i, j: (i, 0)),
                  pl.BlockSpec((k, tn), lambda i, j: (0, j)),
                  pl.BlockSpec((tm, tn), lambda i, j: (i, j)),
                  pl.BlockSpec((1, n), lambda i, j: (0, 0))],
        out_specs=[pl.BlockSpec((tm, tn), lambda i, j: (i, j)),
                   pl.BlockSpec((tm, n), lambda i, j: (i, 0))],
        out_shape=[jax.ShapeDtypeStruct((m, n), F32), jax.ShapeDtypeStruct((m, n), BF16)],
        scratch_shapes=[pltpu.VMEM((tm, n), F32)],
        compiler_params=_params("parallel", "arbitrary"),
        name=name,
    )(a, w, res, g.reshape(1, n).astype(F32))


def _repack_kernel(wm_ref, wn_ref, o_ref, *, first_shifted, first_halved):
    j = pl.program_id(0)

    @pl.when(j < first_shifted)
    def _():
        o_ref[...] = wm_ref[...].T.astype(o_ref.dtype)

    @pl.when(j >= first_shifted)
    def _():
        gain = jnp.where(j >= first_halved, 0.5, 1.0).astype(F32)
        rows = jnp.concatenate([wm_ref[MLA_ROPE:, :], wn_ref[...]], axis=0)
        o_ref[...] = (rows.T * gain).astype(o_ref.dtype)


def repack_w_in(w_in_t, layer, n_lat, n_halved_from, tn=512):
    _, n_in, k = w_in_t.shape
    n_out = n_in - MLA_ROPE
    assert n_out % tn == 0 and n_lat % tn == 0 and n_halved_from % tn == 0 and tn % MLA_ROPE == 0
    per = tn // MLA_ROPE
    return pl.pallas_call(
        functools.partial(_repack_kernel, first_shifted=n_lat // tn, first_halved=n_halved_from // tn),
        grid=(n_out // tn,),
        in_specs=[pl.BlockSpec((None, tn, k), lambda j: (layer, j, 0)),
                  pl.BlockSpec((None, MLA_ROPE, k), lambda j: (layer, per * (j + 1), 0))],
        out_specs=pl.BlockSpec((k, tn), lambda j: (0, j)),
        out_shape=jax.ShapeDtypeStruct((k, n_out), BF16),
        compiler_params=_params("parallel"),
        name="repack_w_in",
    )(w_in_t, w_in_t)


def _rope_proj_kernel(a_ref, wt_ref, o_ref):
    w = wt_ref[...].astype(BF16)
    o_ref[...] = lax.dot_general(a_ref[...], w, (((1,), (1,)), ((), ())),
                                 preferred_element_type=F32).astype(o_ref.dtype)


def rope_proj(h, w_in_t, layer, col0, tm=1024):
    m, k = h.shape
    return pl.pallas_call(
        _rope_proj_kernel,
        grid=(m // tm,),
        in_specs=[pl.BlockSpec((tm, k), lambda i: (i, 0)),
                  pl.BlockSpec((None, LANES, k), lambda i: (layer, col0 // LANES, 0))],
        out_specs=pl.BlockSpec((tm, LANES), lambda i: (i, 0)),
        out_shape=jax.ShapeDtypeStruct((m, LANES), BF16),
        compiler_params=_params("parallel"),
        name="proj_rope",
    )(h, w_in_t)


def _s5_scan_kernel(u_ref, win_ref, pw_ref, cmat_ref, d_ref, y_ref, h_ref, carry_ref, *, n_tiles, half):
    t = pl.program_id(1)

    @pl.when(t == 0)
    def _():
        carry_ref[...] = jnp.zeros_like(carry_ref)

    u = u_ref[...]
    h_ref[...] = jnp.dot(u, win_ref[0], preferred_element_type=F32)

    def tile(i, carry):
        c_re, c_im = carry
        rows = pl.ds(pl.multiple_of(i * SUBLANES, SUBLANES), SUBLANES)
        x_re = h_ref[rows, :half]
        x_im = h_ref[rows, half:]
        for lvl, shift in enumerate((1, 2, 4)):
            a_re = pw_ref[0, (2 * lvl) * SUBLANES:(2 * lvl + 1) * SUBLANES, :]
            a_im = pw_ref[0, (2 * lvl + 1) * SUBLANES:(2 * lvl + 2) * SUBLANES, :]
            s_re = pltpu.roll(x_re, shift, 0)
            s_im = pltpu.roll(x_im, shift, 0)
            x_re, x_im = (x_re + (a_re * s_re - a_im * s_im),
                          x_im + (a_re * s_im + a_im * s_re))
        p_re = pw_ref[0, 6 * SUBLANES:7 * SUBLANES, :]
        p_im = pw_ref[0, 7 * SUBLANES:8 * SUBLANES, :]
        x_re, x_im = (x_re + (p_re * c_re - p_im * c_im),
                      x_im + (p_re * c_im + p_im * c_re))
        h_ref[rows, :half] = x_re
        h_ref[rows, half:] = x_im
        last = SUBLANES - 1
        return (jnp.broadcast_to(x_re[last:, :], x_re.shape),
                jnp.broadcast_to(x_im[last:, :], x_im.shape))

    c_re, c_im = lax.fori_loop(0, n_tiles, tile, (carry_ref[0], carry_ref[1]), unroll=4)
    carry_ref[0] = c_re
    carry_ref[1] = c_im

    y = jnp.dot(h_ref[...].astype(BF16), cmat_ref[0], preferred_element_type=F32)
    y = y + d_ref[...] * u.astype(F32)
    c0 = math.sqrt(2.0 / math.pi)
    y = 0.5 * y * (1.0 + jnp.tanh(c0 * (y + 0.044715 * (y * y * y))))
    y_ref[...] = y.astype(y_ref.dtype)


def s5_scan(lat, win, pw, cmat, d, tc=512):
    s = lat.shape[0]
    nblk, _, two_half = win.shape
    half = two_half // 2
    w = nblk * LANES
    kern = functools.partial(_s5_scan_kernel, n_tiles=tc // SUBLANES, half=half)
    return pl.pallas_call(
        kern,
        grid=(nblk, s // tc),
        in_specs=[pl.BlockSpec((tc, LANES), lambda b, t: (t, b)),
                  pl.BlockSpec((1, LANES, two_half), lambda b, t: (b, 0, 0)),
                  pl.BlockSpec((1, 8 * SUBLANES, half), lambda b, t: (b, 0, 0)),
                  pl.BlockSpec((1, two_half, LANES), lambda b, t: (b, 0, 0)),
                  pl.BlockSpec((1, LANES), lambda b, t: (0, b))],
        out_specs=pl.BlockSpec((tc, LANES), lambda b, t: (t, b)),
        out_shape=jax.ShapeDtypeStruct((s, w), BF16),
        scratch_shapes=[pltpu.VMEM((tc, two_half), F32),
                        pltpu.VMEM((2, SUBLANES, half), F32)],
        compiler_params=_params("parallel", "arbitrary"),
        name="s5_scan",
    )(lat, win, pw, cmat, d)


def _s5_tables(lam_re, lam_im, log_dt, b_re, b_im, c_re, c_im, d_skip):
    g, p = lam_re.shape
    hg = b_re.shape[-1]
    gpb = LANES // hg
    nblk = g // gpb
    lr = lam_re.astype(F32)
    li = lam_im.astype(F32)
    dt = jnp.exp(log_dt.astype(F32))[:, None]
    mag = jnp.exp(lr * dt)
    ab_re = mag * jnp.cos(li * dt)
    ab_im = mag * jnp.sin(li * dt)
    den = lr * lr + li * li
    nr = ab_re - 1.0
    ni = ab_im
    coef_re = (nr * lr + ni * li) / den
    coef_im = (ni * lr - nr * li) / den
    br = b_re.astype(F32)
    bi = b_im.astype(F32)
    bb_re = coef_re[..., None] * br - coef_im[..., None] * bi
    bb_im = coef_re[..., None] * bi + coef_im[..., None] * br
    eye = jnp.eye(gpb, dtype=F32)

    def block_diag_in(bb):
        v = jnp.transpose(bb, (0, 2, 1)).reshape(nblk, gpb, hg, p)
        return jnp.einsum("bghp,gk->bghkp", v, eye).reshape(nblk, gpb * hg, gpb * p)

    win = jnp.concatenate([block_diag_in(bb_re), block_diag_in(bb_im)], axis=-1).astype(BF16)

    def block_diag_out(c):
        v = jnp.transpose(c.astype(F32), (0, 2, 1)).reshape(nblk, gpb, p, hg)
        return jnp.einsum("bgph,gk->bgpkh", v, eye).reshape(nblk, gpb * p, gpb * hg)

    cmat = jnp.concatenate([block_diag_out(c_re), -block_diag_out(c_im)], axis=1).astype(BF16)

    pows = [(ab_re, ab_im)]
    for _ in range(SUBLANES - 1):
        qr, qi = pows[-1]
        pows.append((qr * ab_re - qi * ab_im, qr * ab_im + qi * ab_re))
    row = jnp.arange(SUBLANES)[:, None]

    def flat(v):
        return v.reshape(nblk, 1, gpb * p)

    parts = []
    for shift in (1, 2, 4):
        keep = (row >= shift).astype(F32)[None]
        parts.append(flat(pows[shift - 1][0]) * keep)
        parts.append(flat(pows[shift - 1][1]) * keep)
    parts.append(jnp.concatenate([flat(pows[r][0]) for r in range(SUBLANES)], axis=1))
    parts.append(jnp.concatenate([flat(pows[r][1]) for r in range(SUBLANES)], axis=1))
    pw = jnp.concatenate(parts, axis=1)
    return win, pw, cmat, d_skip.astype(F32).reshape(1, -1)


def _glu_kernel(y_ref, w_ref, o_ref):
    y = y_ref[...]
    z = jnp.dot(y, w_ref[...], preferred_element_type=F32)
    o_ref[...] = (y.astype(F32) * _sigmoid(z)).astype(o_ref.dtype)


def glu(y, w, tm=512):
    s, n = y.shape
    return pl.pallas_call(
        _glu_kernel,
        grid=(s // tm,),
        in_specs=[pl.BlockSpec((tm, n), lambda i: (i, 0)),
                  pl.BlockSpec((n, n), lambda i: (0, 0))],
        out_specs=pl.BlockSpec((tm, n), lambda i: (i, 0)),
        out_shape=jax.ShapeDtypeStruct((s, n), BF16),
        compiler_params=_params("parallel"),
        name="s5_glu",
    )(y, w)


def _rms(x, g):
    x = x.astype(F32)
    ms = jnp.mean(x * x, axis=-1, keepdims=True)
    return (x * lax.rsqrt(ms + NORM_EPS) * g).astype(BF16)


def _mla_q_kernel(lat_ref, g_ref, w_ref, wrot_ref, cos_ref, sin_ref, q_ref, *, heads, scale):
    h = _rms(lat_ref[...], g_ref[...])
    main = jnp.dot(h, w_ref[...], preferred_element_type=F32)
    rot = jnp.dot(h, wrot_ref[...], preferred_element_type=F32)
    cos = cos_ref[...]
    sin = sin_ref[...]
    for hd in range(heads):
        base = hd * MLA_QK
        q_ref[:, base:base + MLA_NOPE] = (main[:, base:base + MLA_NOPE] * scale).astype(q_ref.dtype)
        r = main[:, base + MLA_NOPE:base + MLA_QK] * cos + rot[:, hd * LANES:(hd + 1) * LANES] * sin
        q_ref[:, base + MLA_NOPE:base + MLA_QK] = (r * scale).astype(q_ref.dtype)


def mla_q_proj(lat, col0, g_q, w_main, w_rot, cos, sin, heads, tm=512):
    s = lat.shape[0]
    rank = w_main.shape[0]
    scale = (MLA_NOPE + MLA_ROPE) ** -0.5 * math.log2(math.e)
    col_blk = col0 // rank
    return pl.pallas_call(
        functools.partial(_mla_q_kernel, heads=heads, scale=scale),
        grid=(s // tm,),
        in_specs=[pl.BlockSpec((tm, rank), lambda i: (i, col_blk)),
                  pl.BlockSpec((1, rank), lambda i: (0, 0)),
                  pl.BlockSpec(w_main.shape, lambda i: (0, 0)),
                  pl.BlockSpec(w_rot.shape, lambda i: (0, 0)),
                  pl.BlockSpec((tm, LANES), lambda i: (i, 0)),
                  pl.BlockSpec((tm, LANES), lambda i: (i, 0))],
        out_specs=pl.BlockSpec((tm, heads * MLA_QK), lambda i: (i, 0)),
        out_shape=jax.ShapeDtypeStruct((s, heads * MLA_QK), BF16),
        compiler_params=_params("parallel"),
        name="mla_q_proj",
    )(lat, g_q, w_main, w_rot, cos, sin)


def _mla_kv_kernel(lat_ref, g_ref, wk_ref, wv_ref, kr_ref, cos_ref, sin_ref, k_ref, v_ref, *, heads):
    h = _rms(lat_ref[...], g_ref[...])
    kn = jnp.dot(h, wk_ref[...], preferred_element_type=F32)
    v_ref[...] = jnp.dot(h, wv_ref[...], preferred_element_type=F32).astype(v_ref.dtype)
    x = kr_ref[...].astype(F32)
    half = MLA_ROPE // 2
    lane = lax.broadcasted_iota(jnp.int32, x.shape, 1)
    rot = jnp.where(lane < half, -pltpu.roll(x, LANES - half, 1), pltpu.roll(x, half, 1))
    kr = (x * cos_ref[...] + rot * sin_ref[...]).astype(k_ref.dtype)
    for hd in range(heads):
        base = hd * MLA_QK
        k_ref[:, base:base + MLA_NOPE] = kn[:, hd * MLA_NOPE:(hd + 1) * MLA_NOPE].astype(k_ref.dtype)
        k_ref[:, base + MLA_NOPE:base + MLA_QK] = kr


def mla_kv_proj(lat, col0, g_kv, w_k, w_v, kr_raw, cos, sin, heads, tm=512):
    s = lat.shape[0]
    rank = w_k.shape[0]
    col_blk = col0 // rank
    return pl.pallas_call(
        functools.partial(_mla_kv_kernel, heads=heads),
        grid=(s // tm,),
        in_specs=[pl.BlockSpec((tm, rank), lambda i: (i, col_blk)),
                  pl.BlockSpec((1, rank), lambda i: (0, 0)),
                  pl.BlockSpec(w_k.shape, lambda i: (0, 0)),
                  pl.BlockSpec(w_v.shape, lambda i: (0, 0)),
                  pl.BlockSpec((tm, LANES), lambda i: (i, 0)),
                  pl.BlockSpec((tm, LANES), lambda i: (i, 0)),
                  pl.BlockSpec((tm, LANES), lambda i: (i, 0))],
        out_specs=[pl.BlockSpec((tm, heads * MLA_QK), lambda i: (i, 0)),
                   pl.BlockSpec((tm, heads * MLA_V), lambda i: (i, 0))],
        out_shape=[jax.ShapeDtypeStruct((s, heads * MLA_QK), BF16),
                   jax.ShapeDtypeStruct((s, heads * MLA_V), BF16)],
        compiler_params=_params("parallel"),
        name="mla_kv_proj",
    )(lat, g_kv, w_k, w_v, kr_raw, cos, sin)


def _mla_attn_kernel(q_ref, k_ref, v_ref, o_ref, *, tq, tk, heads_per_step):
    i = pl.program_id(1)
    q = q_ref[...]
    diag_blocks = tq // tk
    heads = range(heads_per_step)
    qk_cols = [slice(hh * MLA_QK, (hh + 1) * MLA_QK) for hh in heads]
    v_cols = [slice(hh * MLA_V, (hh + 1) * MLA_V) for hh in heads]

    def step(j, carry, masked):
        rows = pl.ds(pl.multiple_of(j * tk, tk), tk)
        k = k_ref[rows, :]
        v = v_ref[rows, :]
        ss = [lax.dot_general(q[:, c], k[:, c], (((1,), (1,)), ((), ())), preferred_element_type=F32)
              for c in qk_cols]
        if masked:
            r = lax.broadcasted_iota(jnp.int32, ss[0].shape, 0) + i * tq
            c = lax.broadcasted_iota(jnp.int32, ss[0].shape, 1) + j * tk
            ss = [jnp.where(c <= r, s, -jnp.inf) for s in ss]
        out = []
        ps = []
        for s, (m, l, acc) in zip(ss, carry):
            m_new = jnp.maximum(m, jnp.max(s, axis=-1, keepdims=True))
            alpha = jnp.exp2(m - m_new)
            p = jnp.exp2(s - m_new)
            ps.append(p.astype(BF16))
            out.append((m_new, alpha * l + jnp.sum(p, axis=-1, keepdims=True), alpha * acc))
        pv = [jnp.dot(p, v[:, c], preferred_element_type=F32) for p, c in zip(ps, v_cols)]
        return tuple((m, l, acc + o) for (m, l, acc), o in zip(out, pv))

    init = tuple((jnp.full((tq, 1), -jnp.inf, F32), jnp.zeros((tq, 1), F32), jnp.zeros((tq, MLA_V), F32))
                 for _ in heads)
    carry = lax.fori_loop(0, i * diag_blocks, lambda j, c: step(j, c, False), init)
    for d in range(diag_blocks):
        carry = step(i * diag_blocks + d, carry, True)
    o_ref[...] = jnp.concatenate([acc / l for _, l, acc in carry], axis=-1).astype(o_ref.dtype)


def mla_attention(q, k, v, heads, tq=1024, tk=1024, heads_per_step=2):
    s = q.shape[0]
    hps = heads_per_step
    return pl.pallas_call(
        functools.partial(_mla_attn_kernel, tq=tq, tk=tk, heads_per_step=hps),
        grid=(heads // hps, s // tq),
        in_specs=[pl.BlockSpec((tq, hps * MLA_QK), lambda h, i: (i, h)),
                  pl.BlockSpec((s, hps * MLA_QK), lambda h, i: (0, h)),
                  pl.BlockSpec((s, hps * MLA_V), lambda h, i: (0, h))],
        out_specs=pl.BlockSpec((tq, hps * MLA_V), lambda h, i: (i, h)),
        out_shape=jax.ShapeDtypeStruct((s, heads * MLA_V), BF16),
        compiler_params=_params("parallel", "arbitrary"),
        name="mla_attention",
    )(q, k, v)


def _sb_attn_kernel(q_ref, k_ref, v_ref, tri_ref, o_ref, acc_ref, rest_ref, *, t, scale, heads_per_step):
    i = pl.program_id(1)
    q = q_ref[...]
    tri = tri_ref[...]
    acc_ref[...] = jnp.zeros_like(acc_ref)
    rest_ref[...] = jnp.zeros_like(rest_ref)
    r_idx = lax.broadcasted_iota(jnp.int32, (t, t), 0)
    c_idx = lax.broadcasted_iota(jnp.int32, (t, t), 1)
    strictly_causal = c_idx < r_idx

    def block(j, diagonal):
        rows = pl.ds(pl.multiple_of(j * t, t), t)
        k = k_ref[rows, :]
        v = v_ref[rows, :]
        acc = acc_ref[...]
        heads = range(heads_per_step)
        cols = [slice(hh * SB_HEAD_DIM, (hh + 1) * SB_HEAD_DIM) for hh in heads]
        zs = [lax.dot_general(q[:, c], k[:, c], (((1,), (1,)), ((), ())), preferred_element_type=F32) * scale
              for c in cols]
        log_betas = [jnp.minimum(z, 0.0) - jnp.log(1.0 + jnp.exp(-jnp.abs(z))) for z in zs]
        log_rests = [lb - z for lb, z in zip(log_betas, zs)]
        if diagonal:
            log_rests = [jnp.where(strictly_causal, lr, 0.0) for lr in log_rests]
        his = [lr.astype(BF16) for lr in log_rests]
        los = [(lr - hi.astype(F32)).astype(BF16) for lr, hi in zip(log_rests, his)]
        laters = [jnp.dot(hi, tri, preferred_element_type=F32) for hi in his]
        laters = [la + jnp.dot(lo, tri, preferred_element_type=F32) for la, lo in zip(laters, los)]
        ws = [jnp.exp(lb + la + rest_ref[hh]) for hh, lb, la in zip(heads, log_betas, laters)]
        if diagonal:
            ws = [jnp.where(strictly_causal, w, 0.0) for w in ws]
        outs = [acc[:, c] + jnp.dot(w.astype(BF16), v[:, c], preferred_element_type=F32)
                for c, w in zip(cols, ws)]
        acc_ref[...] = jnp.concatenate(outs, axis=-1)
        rest_max = None
        for hh, lr in zip(heads, log_rests):
            rest = rest_ref[hh] + jnp.sum(lr, axis=-1, keepdims=True)
            rest_ref[hh] = rest
            head_max = jnp.max(rest)
            rest_max = head_max if rest_max is None else jnp.maximum(rest_max, head_max)
        return rest_max

    def cond(carry):
        j, rest_max = carry
        return jnp.logical_and(j >= 0, rest_max > SB_LOG_WEIGHT_FLOOR)

    def body(carry):
        j, _ = carry
        return j - 1, block(j, False)

    lax.while_loop(cond, body, (i - 1, block(i, True)))
    o_ref[...] = acc_ref[...].astype(o_ref.dtype)


def sb_attention(proj, col0, heads, t=256, heads_per_step=4):
    s = proj.shape[0]
    width = heads_per_step * SB_HEAD_DIM
    groups = heads // heads_per_step
    base = col0 // width
    tri = (jnp.arange(t)[:, None] > jnp.arange(t)[None, :]).astype(BF16)
    return pl.pallas_call(
        functools.partial(_sb_attn_kernel, t=t, scale=SB_HEAD_DIM ** -0.5, heads_per_step=heads_per_step),
        grid=(groups, s // t),
        in_specs=[pl.BlockSpec((t, width), lambda h, i: (i, base + h)),
                  pl.BlockSpec((s, width), lambda h, i: (0, base + groups + h)),
                  pl.BlockSpec((s, width), lambda h, i: (0, base + 2 * groups + h)),
                  pl.BlockSpec((t, t), lambda h, i: (0, 0))],
        out_specs=pl.BlockSpec((t, width), lambda h, i: (i, h)),
        out_shape=jax.ShapeDtypeStruct((s, heads * SB_HEAD_DIM), BF16),
        scratch_shapes=[pltpu.VMEM((t, width), F32), pltpu.VMEM((heads_per_step, t, 1), F32)],
        compiler_params=_params("parallel", "arbitrary"),
        name="sb_attention",
    )(proj, proj, proj, tri)


def _merge_kernel(y0_ref, y1_ref, y2_ref, g0_ref, g1_ref, g2_ref, m0_ref, m1_ref, m2_ref,
                  w_ref, o_ref, ys_ref):
    j = pl.program_id(1)

    @pl.when(j == 0)
    def _():
        for n, (y_ref, g_ref) in enumerate(((y0_ref, g0_ref), (y1_ref, g1_ref), (y2_ref, g2_ref))):
            g = g_ref[...].astype(F32)
            ys_ref[n] = (y_ref[...].astype(F32) * (g * _sigmoid(g))).astype(BF16)

    acc = None
    for n, m_ref in enumerate((m0_ref, m1_ref, m2_ref)):
        b = jnp.dot(ys_ref[n], w_ref[n], preferred_element_type=F32)
        term = b + b * jnp.tanh(m_ref[...].astype(F32))
        acc = term if acc is None else acc + term
    o_ref[...] = acc.astype(o_ref.dtype)


def gated_merge(ys, proj, col0, half_w_branch, d_model, tm=512, tn=1024):
    s, w = ys[0].shape
    gate_col0 = col0 + 3 * w
    merge_col0 = gate_col0 + N_BRANCH * w
    y_spec = pl.BlockSpec((tm, w), lambda i, j: (i, 0))

    def gate_spec(n):
        return pl.BlockSpec((tm, w), lambda i, j: (i, gate_col0 // w + n))

    def merge_spec(n):
        return pl.BlockSpec((tm, tn), lambda i, j: (i, (merge_col0 + n * d_model) // tn + j))

    return pl.pallas_call(
        _merge_kernel,
        grid=(s // tm, d_model // tn),
        in_specs=[y_spec, y_spec, y_spec,
                  gate_spec(0), gate_spec(1), gate_spec(2),
                  merge_spec(0), merge_spec(1), merge_spec(2),
                  pl.BlockSpec((N_BRANCH, w, tn), lambda i, j: (0, 0, j))],
        out_specs=pl.BlockSpec((tm, tn), lambda i, j: (i, j)),
        out_shape=jax.ShapeDtypeStruct((s, d_model), BF16),
        scratch_shapes=[pltpu.VMEM((N_BRANCH, tm, w), BF16)],
        compiler_params=_params("parallel", "arbitrary"),
        name="gated_merge",
    )(ys[0], ys[1], ys[2], proj, proj, proj, proj, proj, proj, half_w_branch)


def _ple_kernel(h_ref, wg_ref, p_ref, wp_ref, x_ref, g_ref, *out_and_scratch, tn, emit_x):
    if emit_x:
        o_ref, n_ref, row_ref = out_and_scratch
    else:
        n_ref, row_ref = out_and_scratch
    gate = _sigmoid(jnp.dot(h_ref[...], wg_ref[...], preferred_element_type=F32))
    e = jnp.dot(p_ref[...].astype(BF16), wp_ref[...], preferred_element_type=F32)
    x = x_ref[...] + gate * e
    if emit_x:
        o_ref[...] = x
    _store_row_and_norm(x, pl.program_id(1), row_ref, g_ref, n_ref, tn)


def ple_gate(h, w_gate, p, w_proj, x, g_next, norm_dtype, emit_x, tm=512, tn=1024):
    s, d = x.shape
    pd = p.shape[1]
    x_spec = pl.BlockSpec((tm, tn), lambda i, j: (i, j))
    n_spec = pl.BlockSpec((tm, d), lambda i, j: (i, 0))
    n_shape = jax.ShapeDtypeStruct((s, d), norm_dtype)
    return pl.pallas_call(
        functools.partial(_ple_kernel, tn=tn, emit_x=emit_x),
        grid=(s // tm, d // tn),
        in_specs=[pl.BlockSpec((tm, d), lambda i, j: (i, 0)),
                  pl.BlockSpec((d, tn), lambda i, j: (0, j)),
                  pl.BlockSpec((tm, pd), lambda i, j: (i, 0)),
                  pl.BlockSpec((pd, tn), lambda i, j: (0, j)),
                  x_spec,
                  pl.BlockSpec((1, d), lambda i, j: (0, 0))],
        out_specs=[x_spec, n_spec] if emit_x else [n_spec],
        out_shape=[jax.ShapeDtypeStruct((s, d), F32), n_shape] if emit_x else [n_shape],
        scratch_shapes=[pltpu.VMEM((tm, d), F32)],
        compiler_params=_params("parallel", "arbitrary"),
        name="ple_gate",
    )(h, w_gate, p, w_proj, x, g_next.reshape(1, d).astype(F32))


def _rot_half_cols(w):
    half = MLA_ROPE // 2
    return jnp.concatenate([-w[..., half:], w[..., :half]], axis=-1)


def _mla_weights(w_uq, w_ukv, heads):
    rank = w_uq.shape[0]
    wq = w_uq.reshape(rank, heads, MLA_NOPE + MLA_ROPE)
    zeros = jnp.zeros((rank, heads, MLA_QK - MLA_NOPE - MLA_ROPE), w_uq.dtype)
    w_main = jnp.concatenate([wq, zeros], axis=-1).reshape(rank, heads * MLA_QK).astype(BF16)
    w_rot = jnp.concatenate([_rot_half_cols(wq[..., MLA_NOPE:]), zeros], axis=-1)
    w_rot = w_rot.reshape(rank, heads * LANES).astype(BF16)
    wkv = w_ukv.reshape(w_ukv.shape[0], heads, MLA_NOPE + MLA_V)
    w_k = wkv[..., :MLA_NOPE].reshape(-1, heads * MLA_NOPE).astype(BF16)
    w_v = wkv[..., MLA_NOPE:].reshape(-1, heads * MLA_V).astype(BF16)
    return w_main, w_rot, w_k, w_v


def _layer(x, h, p_i, w_in_t, layer, s5, w_glu, g_q, g_kv, w_uq, w_ukv, w_branch, w_out, ple_g, w_ple_gate,
           w_ple_proj, cos, sin, g_next, last):
    s, d = x.shape
    w = d // 2
    q_rank = g_q.shape[0]
    kv_rank = g_kv.shape[0]
    heads = w // MLA_V
    n_lat = w + q_rank + kv_rank

    w_packed = repack_w_in(w_in_t, layer, n_lat, n_lat + 3 * w + N_BRANCH * w)
    proj = matmul(h, w_packed, BF16, tm=1024, tn=1024, name="proj_in")
    kr_raw = rope_proj(h, w_in_t, layer, n_lat)

    win, pw, cmat, d_skip = s5
    y_ssm = glu(s5_scan(proj, win, pw, cmat, d_skip), w_glu.astype(BF16))

    w_main, w_rot, w_k, w_v = _mla_weights(w_uq, w_ukv, heads)
    q = mla_q_proj(proj, w, g_q.reshape(1, -1).astype(F32), w_main, w_rot, cos, sin, heads)
    k, v = mla_kv_proj(proj, w + q_rank, g_kv.reshape(1, -1).astype(F32), w_k, w_v, kr_raw, cos, sin, heads)
    y_mla = mla_attention(q, k, v, heads)

    y_sb = sb_attention(proj, n_lat, heads)

    merged = gated_merge((y_ssm, y_mla, y_sb), proj, n_lat, (0.5 * w_branch).astype(BF16), d)
    x, hp = matmul_residual_norm(merged, w_out.astype(BF16), x, ple_g, tm=512, tn=1024, name="out_proj")

    outs = ple_gate(hp, w_ple_gate.astype(BF16), p_i, w_ple_proj.astype(BF16), x, g_next,
                    F32 if last else BF16, emit_x=not last)
    return (None, outs[0]) if last else (outs[0], outs[1])


def kernel(x, p, ln_g, w_in, ssm_lam_re, ssm_lam_im, ssm_log_dt, ssm_b_re, ssm_b_im, ssm_c_re, ssm_c_im,
           ssm_d, ssm_w_glu, mla_g_q, mla_g_kv, mla_w_uq, mla_w_ukv, w_branch, w_out, ple_g, w_ple_gate,
           w_ple_proj, final_g):
    bsz, seqlen, d = x.shape
    depth = w_in.shape[0]
    pos = jnp.arange(seqlen, dtype=F32)
    inv_freq = ROPE_THETA ** (-jnp.arange(0, MLA_ROPE, 2, dtype=F32) / MLA_ROPE)
    ang = pos[:, None] * inv_freq[None, :]
    pad = jnp.zeros((seqlen, LANES - MLA_ROPE), F32)
    cos = jnp.concatenate([jnp.cos(ang), jnp.cos(ang), pad], axis=-1)
    sin = jnp.concatenate([jnp.sin(ang), jnp.sin(ang), pad], axis=-1)
    w_in_t = jnp.swapaxes(w_in, 1, 2)

    outs = []
    for b in range(bsz):
        xb = x[b]
        hb = rmsnorm(xb, ln_g[0], BF16)
        for i in range(depth):
            last = i == depth - 1
            s5 = _s5_tables(ssm_lam_re[i], ssm_lam_im[i], ssm_log_dt[i], ssm_b_re[i], ssm_b_im[i],
                            ssm_c_re[i], ssm_c_im[i], ssm_d[i])
            xb, hb = _layer(xb, hb, p[i, b], w_in_t, i, s5, ssm_w_glu[i], mla_g_q[i], mla_g_kv[i],
                            mla_w_uq[i], mla_w_ukv[i], w_branch[i], w_out[i], ple_g[i], w_ple_gate[i],
                            w_ple_proj[i], cos, sin, final_g if last else ln_g[i + 1], last)
        outs.append(hb.astype(x.dtype))
    return jnp.stack(outs, axis=0)
```

```python
import functools
import math

import jax
import jax.numpy as jnp
from jax import lax
from jax.experimental import pallas as pl
from jax.experimental.pallas import tpu as pltpu

F32 = jnp.float32
BF16 = jnp.bfloat16

NORM_EPS = 1e-6
ROPE_THETA = 10000.0
DT_GROUP = 16
SSM_STATE = 64
MLA_NOPE = 128
MLA_ROPE = 64
MLA_V = 128
MLA_QK = 256
SB_HEAD_DIM = 128
N_BRANCH = 3

LANES = 128
SUBLANES = 8
VMEM_LIMIT_BYTES = 56 * 1024 * 1024

SB_LOG_WEIGHT_FLOOR = -104.0


def _params(*sem):
    return pltpu.CompilerParams(dimension_semantics=sem, vmem_limit_bytes=VMEM_LIMIT_BYTES)


def _sigmoid(x):
    return 0.5 * (1.0 + jnp.tanh(0.5 * x))


def _silu(x):
    x = x.astype(F32)
    return x * _sigmoid(x)


def _rmsnorm_kernel(x_ref, g_ref, o_ref):
    x = x_ref[...].astype(F32)
    ms = jnp.mean(x * x, axis=-1, keepdims=True)
    o_ref[...] = (x * lax.rsqrt(ms + NORM_EPS) * g_ref[...]).astype(o_ref.dtype)


def rmsnorm(x, g, out_dtype, tm=512):
    m, d = x.shape
    return pl.pallas_call(
        _rmsnorm_kernel,
        grid=(m // tm,),
        in_specs=[pl.BlockSpec((tm, d), lambda i: (i, 0)),
                  pl.BlockSpec((1, d), lambda i: (0, 0))],
        out_specs=pl.BlockSpec((tm, d), lambda i: (i, 0)),
        out_shape=jax.ShapeDtypeStruct((m, d), out_dtype),
        compiler_params=_params("parallel"),
        name="rmsnorm",
    )(x, g.reshape(1, d).astype(F32))


def _matmul_kernel(a_ref, w_ref, o_ref):
    o_ref[...] = jnp.dot(a_ref[...], w_ref[...], preferred_element_type=F32).astype(o_ref.dtype)


def matmul(a, w, out_dtype, tm, tn, name):
    m, k = a.shape
    n = w.shape[1]
    return pl.pallas_call(
        _matmul_kernel,
        grid=(m // tm, n // tn),
        in_specs=[pl.BlockSpec((tm, k), lambda i, j: (i, 0)),
                  pl.BlockSpec((k, tn), lambda i, j: (0, j))],
        out_specs=pl.BlockSpec((tm, tn), lambda i, j: (i, j)),
        out_shape=jax.ShapeDtypeStruct((m, n), out_dtype),
        compiler_params=_params("parallel", "arbitrary"),
        name=name,
    )(a, w)


def _store_row_and_norm(x, j, row_ref, g_ref, n_ref, tn):
    row_ref[:, pl.ds(pl.multiple_of(j * tn, tn), tn)] = x

    @pl.when(j == pl.num_programs(1) - 1)
    def _():
        row = row_ref[...]
        ms = jnp.mean(row * row, axis=-1, keepdims=True)
        n_ref[...] = (row * lax.rsqrt(ms + NORM_EPS) * g_ref[...]).astype(n_ref.dtype)


def _matmul_residual_norm_kernel(a_ref, w_ref, r_ref, g_ref, x_ref, n_ref, row_ref, *, tn):
    x = r_ref[...] + jnp.dot(a_ref[...], w_ref[...], preferred_element_type=F32)
    x_ref[...] = x
    _store_row_and_norm(x, pl.program_id(1), row_ref, g_ref, n_ref, tn)


def matmul_residual_norm(a, w, res, g, tm, tn, name):
    m, k = a.shape
    n = w.shape[1]
    return pl.pallas_call(
        functools.partial(_matmul_residual_norm_kernel, tn=tn),
        grid=(m // tm, n // tn),
        in_specs=[pl.BlockSpec((tm, k), lambda i, j: (i, 0)),
                  pl.BlockSpec((k, tn), lambda i, j: (0, j)),
                  pl.BlockSpec((tm, tn), lambda i, j: (i, j)),
                  pl.BlockSpec((1, n), lambda i, j: (0, 0))],
        out_specs=[pl.BlockSpec((tm, tn), lambda i, j: (i, j)),
                   pl.BlockSpec((tm, n), lambda i, j: (i, 0))],
        out_shape=[jax.ShapeDtypeStruct((m, n), F32), jax.ShapeDtypeStruct((m, n), BF16)],
        scratch_shapes=[pltpu.VMEM((tm, n), F32)],
        compiler_params=_params("parallel", "arbitrary"),
        name=name,
    )(a, w, res, g.reshape(1, n).astype(F32))


def _repack_kernel(wm_ref, wn_ref, o_ref, *, first_shifted, first_halved):
    j = pl.program_id(0)

    @pl.when(j < first_shifted)
    def _():
        o_ref[...] = wm_ref[...].T.astype(o_ref.dtype)

    @pl.when(j >= first_shifted)
    def _():
        gain = jnp.where(j >= first_halved, 0.5, 1.0).astype(F32)
        rows = jnp.concatenate([wm_ref[MLA_ROPE:, :], wn_ref[...]], axis=0)
        o_ref[...] = (rows.T * gain).astype(o_ref.dtype)


def repack_w_in(w_in_t, layer, n_lat, n_halved_from, tn=512):
    _, n_in, k = w_in_t.shape
    n_out = n_in - MLA_ROPE
    assert n_out % tn == 0 and n_lat % tn == 0 and n_halved_from % tn == 0 and tn % MLA_ROPE == 0
    per = tn // MLA_ROPE
    return pl.pallas_call(
        functools.partial(_repack_kernel, first_shifted=n_lat // tn, first_halved=n_halved_from // tn),
        grid=(n_out // tn,),
        in_specs=[pl.BlockSpec((None, tn, k), lambda j: (layer, j, 0)),
                  pl.BlockSpec((None, MLA_ROPE, k), lambda j: (layer, per * (j + 1), 0))],
        out_specs=pl.BlockSpec((k, tn), lambda j: (0, j)),
        out_shape=jax.ShapeDtypeStruct((k, n_out), BF16),
        compiler_params=_params("parallel"),
        name="repack_w_in",
    )(w_in_t, w_in_t)


def _rope_proj_kernel(a_ref, wt_ref, o_ref):
    w = wt_ref[...].astype(BF16)
    o_ref[...] = lax.dot_general(a_ref[...], w, (((1,), (1,)), ((), ())),
                                 preferred_element_type=F32).astype(o_ref.dtype)


def rope_proj(h, w_in_t, layer, col0, tm=1024):
    m, k = h.shape
    return pl.pallas_call(
        _rope_proj_kernel,
        grid=(m // tm,),
        in_specs=[pl.BlockSpec((tm, k), lambda i: (i, 0)),
                  pl.BlockSpec((None, LANES, k), lambda i: (layer, col0 // LANES, 0))],
        out_specs=pl.BlockSpec((tm, LANES), lambda i: (i, 0)),
        out_shape=jax.ShapeDtypeStruct((m, LANES), BF16),
        compiler_params=_params("parallel"),
        name="proj_rope",
    )(h, w_in_t)


def _s5_scan_kernel(u_ref, win_ref, pw_ref, cmat_ref, d_ref, y_ref, h_ref, carry_ref, *, n_tiles, half):
    t = pl.program_id(1)

    @pl.when(t == 0)
    def _():
        carry_ref[...] = jnp.zeros_like(carry_ref)

    u = u_ref[...]
    h_ref[...] = jnp.dot(u, win_ref[0], preferred_element_type=F32)

    def tile(i, carry):
        c_re, c_im = carry
        rows = pl.ds(pl.multiple_of(i * SUBLANES, SUBLANES), SUBLANES)
        x_re = h_ref[rows, :half]
        x_im = h_ref[rows, half:]
        for lvl, shift in enumerate((1, 2, 4)):
            a_re = pw_ref[0, (2 * lvl) * SUBLANES:(2 * lvl + 1) * SUBLANES, :]
            a_im = pw_ref[0, (2 * lvl + 1) * SUBLANES:(2 * lvl + 2) * SUBLANES, :]
            s_re = pltpu.roll(x_re, shift, 0)
            s_im = pltpu.roll(x_im, shift, 0)
            x_re, x_im = (x_re + (a_re * s_re - a_im * s_im),
                          x_im + (a_re * s_im + a_im * s_re))
        p_re = pw_ref[0, 6 * SUBLANES:7 * SUBLANES, :]
        p_im = pw_ref[0, 7 * SUBLANES:8 * SUBLANES, :]
        x_re, x_im = (x_re + (p_re * c_re - p_im * c_im),
                      x_im + (p_re * c_im + p_im * c_re))
        h_ref[rows, :half] = x_re
        h_ref[rows, half:] = x_im
        last = SUBLANES - 1
        return (jnp.broadcast_to(x_re[last:, :], x_re.shape),
                jnp.broadcast_to(x_im[last:, :], x_im.shape))

    c_re, c_im = lax.fori_loop(0, n_tiles, tile, (carry_ref[0], carry_ref[1]), unroll=8)
    carry_ref[0] = c_re
    carry_ref[1] = c_im

    y = jnp.dot(h_ref[...].astype(BF16), cmat_ref[0], preferred_element_type=F32)
    y = y + d_ref[...] * u.astype(F32)
    c0 = math.sqrt(2.0 / math.pi)
    y = 0.5 * y * (1.0 + jnp.tanh(c0 * (y + 0.044715 * (y * y * y))))
    y_ref[...] = y.astype(y_ref.dtype)


def s5_scan(lat, win, pw, cmat, d, tc=1024):
    s = lat.shape[0]
    nblk, _, two_half = win.shape
    half = two_half // 2
    w = nblk * LANES
    kern = functools.partial(_s5_scan_kernel, n_tiles=tc // SUBLANES, half=half)
    return pl.pallas_call(
        kern,
        grid=(nblk, s // tc),
        in_specs=[pl.BlockSpec((tc, LANES), lambda b, t: (t, b)),
                  pl.BlockSpec((1, LANES, two_half), lambda b, t: (b, 0, 0)),
                  pl.BlockSpec((1, 8 * SUBLANES, half), lambda b, t: (b, 0, 0)),
                  pl.BlockSpec((1, two_half, LANES), lambda b, t: (b, 0, 0)),
                  pl.BlockSpec((1, LANES), lambda b, t: (0, b))],
        out_specs=pl.BlockSpec((tc, LANES), lambda b, t: (t, b)),
        out_shape=jax.ShapeDtypeStruct((s, w), BF16),
        scratch_shapes=[pltpu.VMEM((tc, two_half), F32),
                        pltpu.VMEM((2, SUBLANES, half), F32)],
        compiler_params=_params("parallel", "arbitrary"),
        name="s5_scan",
    )(lat, win, pw, cmat, d)


def _s5_tables(lam_re, lam_im, log_dt, b_re, b_im, c_re, c_im, d_skip):
    g, p = lam_re.shape
    hg = b_re.shape[-1]
    gpb = LANES // hg
    nblk = g // gpb
    lr = lam_re.astype(F32)
    li = lam_im.astype(F32)
    dt = jnp.exp(log_dt.astype(F32))[:, None]
    mag = jnp.exp(lr * dt)
    ab_re = mag * jnp.cos(li * dt)
    ab_im = mag * jnp.sin(li * dt)
    den = lr * lr + li * li
    nr = ab_re - 1.0
    ni = ab_im
    coef_re = (nr * lr + ni * li) / den
    coef_im = (ni * lr - nr * li) / den
    br = b_re.astype(F32)
    bi = b_im.astype(F32)
    bb_re = coef_re[..., None] * br - coef_im[..., None] * bi
    bb_im = coef_re[..., None] * bi + coef_im[..., None] * br
    eye = jnp.eye(gpb, dtype=F32)

    def block_diag_in(bb):
        v = jnp.transpose(bb, (0, 2, 1)).reshape(nblk, gpb, hg, p)
        return jnp.einsum("bghp,gk->bghkp", v, eye).reshape(nblk, gpb * hg, gpb * p)

    win = jnp.concatenate([block_diag_in(bb_re), block_diag_in(bb_im)], axis=-1).astype(BF16)

    def block_diag_out(c):
        v = jnp.transpose(c.astype(F32), (0, 2, 1)).reshape(nblk, gpb, p, hg)
        return jnp.einsum("bgph,gk->bgpkh", v, eye).reshape(nblk, gpb * p, gpb * hg)

    cmat = jnp.concatenate([block_diag_out(c_re), -block_diag_out(c_im)], axis=1).astype(BF16)

    pows = [(ab_re, ab_im)]
    for _ in range(SUBLANES - 1):
        qr, qi = pows[-1]
        pows.append((qr * ab_re - qi * ab_im, qr * ab_im + qi * ab_re))
    row = jnp.arange(SUBLANES)[:, None]

    def flat(v):
        return v.reshape(nblk, 1, gpb * p)

    parts = []
    for shift in (1, 2, 4):
        keep = (row >= shift).astype(F32)[None]
        parts.append(flat(pows[shift - 1][0]) * keep)
        parts.append(flat(pows[shift - 1][1]) * keep)
    parts.append(jnp.concatenate([flat(pows[r][0]) for r in range(SUBLANES)], axis=1))
    parts.append(jnp.concatenate([flat(pows[r][1]) for r in range(SUBLANES)], axis=1))
    pw = jnp.concatenate(parts, axis=1)
    return win, pw, cmat, d_skip.astype(F32).reshape(1, -1)


def _glu_kernel(y_ref, w_ref, g_ref, o_ref):
    y = y_ref[...]
    z = jnp.dot(y, w_ref[...], preferred_element_type=F32)
    o_ref[...] = (y.astype(F32) * _sigmoid(z) * _silu(g_ref[...])).astype(o_ref.dtype)


def glu(y, w, proj, gate_col0, tm=512):
    s, n = y.shape
    return pl.pallas_call(
        _glu_kernel,
        grid=(s // tm,),
        in_specs=[pl.BlockSpec((tm, n), lambda i: (i, 0)),
                  pl.BlockSpec((n, n), lambda i: (0, 0)),
                  pl.BlockSpec((tm, n), lambda i: (i, gate_col0 // n))],
        out_specs=pl.BlockSpec((tm, n), lambda i: (i, 0)),
        out_shape=jax.ShapeDtypeStruct((s, n), BF16),
        compiler_params=_params("parallel"),
        name="s5_glu",
    )(y, w, proj)


def _rms(x, g):
    x = x.astype(F32)
    ms = jnp.mean(x * x, axis=-1, keepdims=True)
    return (x * lax.rsqrt(ms + NORM_EPS) * g).astype(BF16)


def _mla_q_kernel(lat_ref, g_ref, w_ref, wrot_ref, cos_ref, sin_ref, q_ref, *, heads, scale):
    h = _rms(lat_ref[...], g_ref[...])
    main = jnp.dot(h, w_ref[...], preferred_element_type=F32)
    rot = jnp.dot(h, wrot_ref[...], preferred_element_type=F32)
    cos = cos_ref[...]
    sin = sin_ref[...]
    for hd in range(heads):
        base = hd * MLA_QK
        q_ref[:, base:base + MLA_NOPE] = (main[:, base:base + MLA_NOPE] * scale).astype(q_ref.dtype)
        r = main[:, base + MLA_NOPE:base + MLA_QK] * cos + rot[:, hd * LANES:(hd + 1) * LANES] * sin
        q_ref[:, base + MLA_NOPE:base + MLA_QK] = (r * scale).astype(q_ref.dtype)


def mla_q_proj(lat, col0, g_q, w_main, w_rot, cos, sin, heads, tm=512):
    s = lat.shape[0]
    rank = w_main.shape[0]
    scale = (MLA_NOPE + MLA_ROPE) ** -0.5 * math.log2(math.e)
    col_blk = col0 // rank
    return pl.pallas_call(
        functools.partial(_mla_q_kernel, heads=heads, scale=scale),
        grid=(s // tm,),
        in_specs=[pl.BlockSpec((tm, rank), lambda i: (i, col_blk)),
                  pl.BlockSpec((1, rank), lambda i: (0, 0)),
                  pl.BlockSpec(w_main.shape, lambda i: (0, 0)),
                  pl.BlockSpec(w_rot.shape, lambda i: (0, 0)),
                  pl.BlockSpec((tm, LANES), lambda i: (i, 0)),
                  pl.BlockSpec((tm, LANES), lambda i: (i, 0))],
        out_specs=pl.BlockSpec((tm, heads * MLA_QK), lambda i: (i, 0)),
        out_shape=jax.ShapeDtypeStruct((s, heads * MLA_QK), BF16),
        compiler_params=_params("parallel"),
        name="mla_q_proj",
    )(lat, g_q, w_main, w_rot, cos, sin)


def _mla_kv_kernel(lat_ref, g_ref, wk_ref, wv_ref, kr_ref, cos_ref, sin_ref, k_ref, v_ref, *, heads):
    h = _rms(lat_ref[...], g_ref[...])
    kn = jnp.dot(h, wk_ref[...], preferred_element_type=F32)
    v_ref[...] = jnp.dot(h, wv_ref[...], preferred_element_type=F32).astype(v_ref.dtype)
    x = kr_ref[...].astype(F32)
    half = MLA_ROPE // 2
    lane = lax.broadcasted_iota(jnp.int32, x.shape, 1)
    rot = jnp.where(lane < half, -pltpu.roll(x, LANES - half, 1), pltpu.roll(x, half, 1))
    kr = (x * cos_ref[...] + rot * sin_ref[...]).astype(k_ref.dtype)
    for hd in range(heads):
        base = hd * MLA_QK
        k_ref[:, base:base + MLA_NOPE] = kn[:, hd * MLA_NOPE:(hd + 1) * MLA_NOPE].astype(k_ref.dtype)
        k_ref[:, base + MLA_NOPE:base + MLA_QK] = kr


def mla_kv_proj(lat, col0, g_kv, w_k, w_v, kr_raw, cos, sin, heads, tm=512):
    s = lat.shape[0]
    rank = w_k.shape[0]
    col_blk = col0 // rank
    return pl.pallas_call(
        functools.partial(_mla_kv_kernel, heads=heads),
        grid=(s // tm,),
        in_specs=[pl.BlockSpec((tm, rank), lambda i: (i, col_blk)),
                  pl.BlockSpec((1, rank), lambda i: (0, 0)),
                  pl.BlockSpec(w_k.shape, lambda i: (0, 0)),
                  pl.BlockSpec(w_v.shape, lambda i: (0, 0)),
                  pl.BlockSpec((tm, LANES), lambda i: (i, 0)),
                  pl.BlockSpec((tm, LANES), lambda i: (i, 0)),
                  pl.BlockSpec((tm, LANES), lambda i: (i, 0))],
        out_specs=[pl.BlockSpec((tm, heads * MLA_QK), lambda i: (i, 0)),
                   pl.BlockSpec((tm, heads * MLA_V), lambda i: (i, 0))],
        out_shape=[jax.ShapeDtypeStruct((s, heads * MLA_QK), BF16),
                   jax.ShapeDtypeStruct((s, heads * MLA_V), BF16)],
        compiler_params=_params("parallel"),
        name="mla_kv_proj",
    )(lat, g_kv, w_k, w_v, kr_raw, cos, sin)


def _mla_attn_kernel(q_ref, k_ref, v_ref, g_ref, o_ref, *, tq, tk, heads_per_step):
    i = pl.program_id(1)
    q = q_ref[...]
    diag_blocks = tq // tk
    heads = range(heads_per_step)
    qk_cols = [slice(hh * MLA_QK, (hh + 1) * MLA_QK) for hh in heads]
    v_cols = [slice(hh * MLA_V, (hh + 1) * MLA_V) for hh in heads]

    def step(j, carry, masked):
        rows = pl.ds(pl.multiple_of(j * tk, tk), tk)
        k = k_ref[rows, :]
        v = v_ref[rows, :]
        ss = [lax.dot_general(q[:, c], k[:, c], (((1,), (1,)), ((), ())), preferred_element_type=F32)
              for c in qk_cols]
        if masked:
            r = lax.broadcasted_iota(jnp.int32, ss[0].shape, 0) + i * tq
            c = lax.broadcasted_iota(jnp.int32, ss[0].shape, 1) + j * tk
            ss = [jnp.where(c <= r, s, -jnp.inf) for s in ss]
        out = []
        ps = []
        for s, (m, l, acc) in zip(ss, carry):
            m_new = jnp.maximum(m, jnp.max(s, axis=-1, keepdims=True))
            alpha = jnp.exp2(m - m_new)
            p = jnp.exp2(s - m_new)
            ps.append(p.astype(BF16))
            out.append((m_new, alpha * l + jnp.sum(p, axis=-1, keepdims=True), alpha * acc))
        pv = [jnp.dot(p, v[:, c], preferred_element_type=F32) for p, c in zip(ps, v_cols)]
        return tuple((m, l, acc + o) for (m, l, acc), o in zip(out, pv))

    init = tuple((jnp.full((tq, 1), -jnp.inf, F32), jnp.zeros((tq, 1), F32), jnp.zeros((tq, MLA_V), F32))
                 for _ in heads)
    carry = lax.fori_loop(0, i * diag_blocks, lambda j, c: step(j, c, False), init)
    for d in range(diag_blocks):
        carry = step(i * diag_blocks + d, carry, True)
    out = jnp.concatenate([acc / l for _, l, acc in carry], axis=-1)
    o_ref[...] = (out * _silu(g_ref[...])).astype(o_ref.dtype)


def mla_attention(q, k, v, proj, gate_col0, heads, tq=1024, tk=1024, heads_per_step=2):
    s = q.shape[0]
    hps = heads_per_step
    gate_blk = gate_col0 // (hps * MLA_V)
    return pl.pallas_call(
        functools.partial(_mla_attn_kernel, tq=tq, tk=tk, heads_per_step=hps),
        grid=(heads // hps, s // tq),
        in_specs=[pl.BlockSpec((tq, hps * MLA_QK), lambda h, i: (i, h)),
                  pl.BlockSpec((s, hps * MLA_QK), lambda h, i: (0, h)),
                  pl.BlockSpec((s, hps * MLA_V), lambda h, i: (0, h)),
                  pl.BlockSpec((tq, hps * MLA_V), lambda h, i: (i, gate_blk + h))],
        out_specs=pl.BlockSpec((tq, hps * MLA_V), lambda h, i: (i, h)),
        out_shape=jax.ShapeDtypeStruct((s, heads * MLA_V), BF16),
        compiler_params=_params("parallel", "arbitrary"),
        name="mla_attention",
    )(q, k, v, proj)


def _sb_attn_kernel(q_ref, k_ref, v_ref, tri_ref, g_ref, o_ref, acc_ref, rest_ref, *, t, scale, heads_per_step):
    i = pl.program_id(1)
    q = q_ref[...]
    tri = tri_ref[...]
    acc_ref[...] = jnp.zeros_like(acc_ref)
    rest_ref[...] = jnp.zeros_like(rest_ref)
    r_idx = lax.broadcasted_iota(jnp.int32, (t, t), 0)
    c_idx = lax.broadcasted_iota(jnp.int32, (t, t), 1)
    strictly_causal = c_idx < r_idx

    def block(j, diagonal):
        rows = pl.ds(pl.multiple_of(j * t, t), t)
        k = k_ref[rows, :]
        v = v_ref[rows, :]
        acc = acc_ref[...]
        heads = range(heads_per_step)
        cols = [slice(hh * SB_HEAD_DIM, (hh + 1) * SB_HEAD_DIM) for hh in heads]
        zs = [lax.dot_general(q[:, c], k[:, c], (((1,), (1,)), ((), ())), preferred_element_type=F32) * scale
              for c in cols]
        log_betas = [jnp.minimum(z, 0.0) - jnp.log(1.0 + jnp.exp(-jnp.abs(z))) for z in zs]
        log_rests = [lb - z for lb, z in zip(log_betas, zs)]
        if diagonal:
            log_rests = [jnp.where(strictly_causal, lr, 0.0) for lr in log_rests]
        his = [lr.astype(BF16) for lr in log_rests]
        los = [(lr - hi.astype(F32)).astype(BF16) for lr, hi in zip(log_rests, his)]
        laters = [jnp.dot(hi, tri, preferred_element_type=F32) for hi in his]
        laters = [la + jnp.dot(lo, tri, preferred_element_type=F32) for la, lo in zip(laters, los)]
        ws = [jnp.exp(lb + la + rest_ref[hh]) for hh, lb, la in zip(heads, log_betas, laters)]
        if diagonal:
            ws = [jnp.where(strictly_causal, w, 0.0) for w in ws]
        outs = [acc[:, c] + jnp.dot(w.astype(BF16), v[:, c], preferred_element_type=F32)
                for c, w in zip(cols, ws)]
        acc_ref[...] = jnp.concatenate(outs, axis=-1)
        rest_max = None
        for hh, lr in zip(heads, log_rests):
            rest = rest_ref[hh] + jnp.sum(lr, axis=-1, keepdims=True)
            rest_ref[hh] = rest
            head_max = jnp.max(rest)
            rest_max = head_max if rest_max is None else jnp.maximum(rest_max, head_max)
        return rest_max

    def cond(carry):
        j, rest_max = carry
        return jnp.logical_and(j >= 0, rest_max > SB_LOG_WEIGHT_FLOOR)

    def body(carry):
        j, _ = carry
        return j - 1, block(j, False)

    lax.while_loop(cond, body, (i - 1, block(i, True)))
    o_ref[...] = (acc_ref[...] * _silu(g_ref[...])).astype(o_ref.dtype)


def sb_attention(proj, col0, gate_col0, heads, t=256, heads_per_step=4):
    s = proj.shape[0]
    width = heads_per_step * SB_HEAD_DIM
    groups = heads // heads_per_step
    base = col0 // width
    tri = (jnp.arange(t)[:, None] > jnp.arange(t)[None, :]).astype(BF16)
    return pl.pallas_call(
        functools.partial(_sb_attn_kernel, t=t, scale=SB_HEAD_DIM ** -0.5, heads_per_step=heads_per_step),
        grid=(groups, s // t),
        in_specs=[pl.BlockSpec((t, width), lambda h, i: (i, base + h)),
                  pl.BlockSpec((s, width), lambda h, i: (0, base + groups + h)),
                  pl.BlockSpec((s, width), lambda h, i: (0, base + 2 * groups + h)),
                  pl.BlockSpec((t, t), lambda h, i: (0, 0)),
                  pl.BlockSpec((t, width), lambda h, i: (i, gate_col0 // width + h))],
        out_specs=pl.BlockSpec((t, width), lambda h, i: (i, h)),
        out_shape=jax.ShapeDtypeStruct((s, heads * SB_HEAD_DIM), BF16),
        scratch_shapes=[pltpu.VMEM((t, width), F32), pltpu.VMEM((heads_per_step, t, 1), F32)],
        compiler_params=_params("parallel", "arbitrary"),
        name="sb_attention",
    )(proj, proj, proj, tri, proj)


def _merge_kernel(y0_ref, y1_ref, y2_ref, m0_ref, m1_ref, m2_ref, w_ref, o_ref):
    acc = None
    for n, (y_ref, m_ref) in enumerate(((y0_ref, m0_ref), (y1_ref, m1_ref), (y2_ref, m2_ref))):
        b = jnp.dot(y_ref[...], w_ref[n], preferred_element_type=F32)
        term = b + b * jnp.tanh(m_ref[...].astype(F32))
        acc = term if acc is None else acc + term
    o_ref[...] = acc.astype(o_ref.dtype)


def gated_merge(ys, proj, merge_col0, half_w_branch, d_model, tm=512, tn=1024):
    s, w = ys[0].shape
    y_spec = pl.BlockSpec((tm, w), lambda i, j: (i, 0))

    def merge_spec(n):
        return pl.BlockSpec((tm, tn), lambda i, j: (i, (merge_col0 + n * d_model) // tn + j))

    return pl.pallas_call(
        _merge_kernel,
        grid=(s // tm, d_model // tn),
        in_specs=[y_spec, y_spec, y_spec,
                  merge_spec(0), merge_spec(1), merge_spec(2),
                  pl.BlockSpec((N_BRANCH, w, tn), lambda i, j: (0, 0, j))],
        out_specs=pl.BlockSpec((tm, tn), lambda i, j: (i, j)),
        out_shape=jax.ShapeDtypeStruct((s, d_model), BF16),
        compiler_params=_params("parallel", "arbitrary"),
        name="gated_merge",
    )(ys[0], ys[1], ys[2], proj, proj, proj, half_w_branch)


def _ple_kernel(h_ref, wg_ref, p_ref, wp_ref, x_ref, g_ref, *out_and_scratch, tn, emit_x):
    if emit_x:
        o_ref, n_ref, row_ref = out_and_scratch
    else:
        n_ref, row_ref = out_and_scratch
    gate = _sigmoid(jnp.dot(h_ref[...], wg_ref[...], preferred_element_type=F32))
    e = jnp.dot(p_ref[...].astype(BF16), wp_ref[...], preferred_element_type=F32)
    x = x_ref[...] + gate * e
    if emit_x:
        o_ref[...] = x
    _store_row_and_norm(x, pl.program_id(1), row_ref, g_ref, n_ref, tn)


def ple_gate(h, w_gate, p, w_proj, x, g_next, norm_dtype, emit_x, tm=512, tn=1024):
    s, d = x.shape
    pd = p.shape[1]
    x_spec = pl.BlockSpec((tm, tn), lambda i, j: (i, j))
    n_spec = pl.BlockSpec((tm, d), lambda i, j: (i, 0))
    n_shape = jax.ShapeDtypeStruct((s, d), norm_dtype)
    return pl.pallas_call(
        functools.partial(_ple_kernel, tn=tn, emit_x=emit_x),
        grid=(s // tm, d // tn),
        in_specs=[pl.BlockSpec((tm, d), lambda i, j: (i, 0)),
                  pl.BlockSpec((d, tn), lambda i, j: (0, j)),
                  pl.BlockSpec((tm, pd), lambda i, j: (i, 0)),
                  pl.BlockSpec((pd, tn), lambda i, j: (0, j)),
                  x_spec,
                  pl.BlockSpec((1, d), lambda i, j: (0, 0))],
        out_specs=[x_spec, n_spec] if emit_x else [n_spec],
        out_shape=[jax.ShapeDtypeStruct((s, d), F32), n_shape] if emit_x else [n_shape],
        scratch_shapes=[pltpu.VMEM((tm, d), F32)],
        compiler_params=_params("parallel", "arbitrary"),
        name="ple_gate",
    )(h, w_gate, p, w_proj, x, g_next.reshape(1, d).astype(F32))


def _rot_half_cols(w):
    half = MLA_ROPE // 2
    return jnp.concatenate([-w[..., half:], w[..., :half]], axis=-1)


def _mla_weights(w_uq, w_ukv, heads):
    rank = w_uq.shape[0]
    wq = w_uq.reshape(rank, heads, MLA_NOPE + MLA_ROPE)
    zeros = jnp.zeros((rank, heads, MLA_QK - MLA_NOPE - MLA_ROPE), w_uq.dtype)
    w_main = jnp.concatenate([wq, zeros], axis=-1).reshape(rank, heads * MLA_QK).astype(BF16)
    w_rot = jnp.concatenate([_rot_half_cols(wq[..., MLA_NOPE:]), zeros], axis=-1)
    w_rot = w_rot.reshape(rank, heads * LANES).astype(BF16)
    wkv = w_ukv.reshape(w_ukv.shape[0], heads, MLA_NOPE + MLA_V)
    w_k = wkv[..., :MLA_NOPE].reshape(-1, heads * MLA_NOPE).astype(BF16)
    w_v = wkv[..., MLA_NOPE:].reshape(-1, heads * MLA_V).astype(BF16)
    return w_main, w_rot, w_k, w_v


def _layer(x, h, p_i, w_in_t, layer, s5, w_glu, g_q, g_kv, w_uq, w_ukv, w_branch, w_out, ple_g, w_ple_gate,
           w_ple_proj, cos, sin, g_next, last):
    s, d = x.shape
    w = d // 2
    q_rank = g_q.shape[0]
    kv_rank = g_kv.shape[0]
    heads = w // MLA_V
    n_lat = w + q_rank + kv_rank

    gate_col0 = n_lat + 3 * w
    merge_col0 = gate_col0 + N_BRANCH * w
    w_packed = repack_w_in(w_in_t, layer, n_lat, merge_col0)
    proj = matmul(h, w_packed, BF16, tm=1024, tn=1024, name="proj_in")
    kr_raw = rope_proj(h, w_in_t, layer, n_lat)

    win, pw, cmat, d_skip = s5
    y_ssm = glu(s5_scan(proj, win, pw, cmat, d_skip), w_glu.astype(BF16), proj, gate_col0)

    w_main, w_rot, w_k, w_v = _mla_weights(w_uq, w_ukv, heads)
    q = mla_q_proj(proj, w, g_q.reshape(1, -1).astype(F32), w_main, w_rot, cos, sin, heads)
    k, v = mla_kv_proj(proj, w + q_rank, g_kv.reshape(1, -1).astype(F32), w_k, w_v, kr_raw, cos, sin, heads)
    y_mla = mla_attention(q, k, v, proj, gate_col0 + w, heads)

    y_sb = sb_attention(proj, n_lat, gate_col0 + 2 * w, heads)

    merged = gated_merge((y_ssm, y_mla, y_sb), proj, merge_col0, (0.5 * w_branch).astype(BF16), d)
    x, hp = matmul_residual_norm(merged, w_out.astype(BF16), x, ple_g, tm=512, tn=1024, name="out_proj")

    outs = ple_gate(hp, w_ple_gate.astype(BF16), p_i, w_ple_proj.astype(BF16), x, g_next,
                    F32 if last else BF16, emit_x=not last)
    return (None, outs[0]) if last else (outs[0], outs[1])


def kernel(x, p, ln_g, w_in, ssm_lam_re, ssm_lam_im, ssm_log_dt, ssm_b_re, ssm_b_im, ssm_c_re, ssm_c_im,
           ssm_d, ssm_w_glu, mla_g_q, mla_g_kv, mla_w_uq, mla_w_ukv, w_branch, w_out, ple_g, w_ple_gate,
           w_ple_proj, final_g):
    bsz, seqlen, d = x.shape
    depth = w_in.shape[0]
    pos = jnp.arange(seqlen, dtype=F32)
    inv_freq = ROPE_THETA ** (-jnp.arange(0, MLA_ROPE, 2, dtype=F32) / MLA_ROPE)
    ang = pos[:, None] * inv_freq[None, :]
    pad = jnp.zeros((seqlen, LANES - MLA_ROPE), F32)
    cos = jnp.concatenate([jnp.cos(ang), jnp.cos(ang), pad], axis=-1)
    sin = jnp.concatenate([jnp.sin(ang), jnp.sin(ang), pad], axis=-1)
    w_in_t = jnp.swapaxes(w_in, 1, 2)

    outs = []
    for b in range(bsz):
        xb = x[b]
        hb = rmsnorm(xb, ln_g[0], BF16)
        for i in range(depth):
            last = i == depth - 1
            s5 = _s5_tables(ssm_lam_re[i], ssm_lam_im[i], ssm_log_dt[i], ssm_b_re[i], ssm_b_im[i],
                            ssm_c_re[i], ssm_c_im[i], ssm_d[i])
            xb, hb = _layer(xb, hb, p[i, b], w_in_t, i, s5, ssm_w_glu[i], mla_g_q[i], mla_g_kv[i],
                            mla_w_uq[i], mla_w_ukv[i], w_branch[i], w_out[i], ple_g[i], w_ple_gate[i],
                            w_ple_proj[i], cos, sin, final_g if last else ln_g[i + 1], last)
        outs.append(hb.astype(x.dtype))
    return jnp.stack(outs, axis=0)
```

```python
import functools
import math

import jax
import jax.numpy as jnp
from jax import lax
from jax.experimental import pallas as pl
from jax.experimental.pallas import tpu as pltpu

F32 = jnp.float32
BF16 = jnp.bfloat16

NORM_EPS = 1e-6
ROPE_THETA = 10000.0
DT_GROUP = 16
SSM_STATE = 64
MLA_NOPE = 128
MLA_ROPE = 64
MLA_V = 128
MLA_QK = 256
SB_HEAD_DIM = 128
N_BRANCH = 3

LANES = 128
SUBLANES = 8
VMEM_LIMIT_BYTES = 56 * 1024 * 1024

SB_LOG_WEIGHT_FLOOR = -104.0


def _params(*sem):
    return pltpu.CompilerParams(dimension_semantics=sem, vmem_limit_bytes=VMEM_LIMIT_BYTES)


def _sigmoid(x):
    return 0.5 * (1.0 + jnp.tanh(0.5 * x))


def _silu(x):
    x = x.astype(F32)
    return x * _sigmoid(x)


def _rmsnorm_kernel(x_ref, g_ref, o_ref):
    x = x_ref[...].astype(F32)
    ms = jnp.mean(x * x, axis=-1, keepdims=True)
    o_ref[...] = (x * lax.rsqrt(ms + NORM_EPS) * g_ref[...]).astype(o_ref.dtype)


def rmsnorm(x, g, out_dtype, tm=512):
    m, d = x.shape
    return pl.pallas_call(
        _rmsnorm_kernel,
        grid=(m // tm,),
        in_specs=[pl.BlockSpec((tm, d), lambda i: (i, 0)),
                  pl.BlockSpec((1, d), lambda i: (0, 0))],
        out_specs=pl.BlockSpec((tm, d), lambda i: (i, 0)),
        out_shape=jax.ShapeDtypeStruct((m, d), out_dtype),
        compiler_params=_params("parallel"),
        name="rmsnorm",
    )(x, g.reshape(1, d).astype(F32))


def _matmul_kernel(a_ref, w_ref, o_ref):
    o_ref[...] = jnp.dot(a_ref[...], w_ref[...], preferred_element_type=F32).astype(o_ref.dtype)


def matmul(a, w, out_dtype, tm, tn, name):
    m, k = a.shape
    n = w.shape[1]
    return pl.pallas_call(
        _matmul_kernel,
        grid=(m // tm, n // tn),
        in_specs=[pl.BlockSpec((tm, k), lambda i, j: (i, 0)),
                  pl.BlockSpec((k, tn), lambda i, j: (0, j))],
        out_specs=pl.BlockSpec((tm, tn), lambda i, j: (i, j)),
        out_shape=jax.ShapeDtypeStruct((m, n), out_dtype),
        compiler_params=_params("parallel", "arbitrary"),
        name=name,
    )(a, w)


def _store_row_and_norm(x, j, row_ref, g_ref, n_ref, tn):
    row_ref[:, pl.ds(pl.multiple_of(j * tn, tn), tn)] = x

    @pl.when(j == pl.num_programs(1) - 1)
    def _():
        row = row_ref[...]
        ms = jnp.mean(row * row, axis=-1, keepdims=True)
        n_ref[...] = (row * lax.rsqrt(ms + NORM_EPS) * g_ref[...]).astype(n_ref.dtype)


def _matmul_residual_norm_kernel(a_ref, w_ref, r_ref, g_ref, x_ref, n_ref, row_ref, *, tn):
    x = r_ref[...] + jnp.dot(a_ref[...], w_ref[...], preferred_element_type=F32)
    x_ref[...] = x
    _store_row_and_norm(x, pl.program_id(1), row_ref, g_ref, n_ref, tn)


def matmul_residual_norm(a, w, res, g, tm, tn, name):
    m, k = a.shape
    n = w.shape[1]
    return pl.pallas_call(
        functools.partial(_matmul_residual_norm_kernel, tn=tn),
        grid=(m // tm, n // tn),
        in_specs=[pl.BlockSpec((tm, k), lambda i, j: (i, 0)),
                  pl.BlockSpec((k, tn), lambda i, j: (0, j)),
                  pl.BlockSpec((tm, tn), lambda i, j: (i, j)),
                  pl.BlockSpec((1, n), lambda i, j: (0, 0))],
        out_specs=[pl.BlockSpec((tm, tn), lambda i, j: (i, j)),
                   pl.BlockSpec((tm, n), lambda i, j: (i, 0))],
        out_shape=[jax.ShapeDtypeStruct((m, n), F32), jax.ShapeDtypeStruct((m, n), BF16)],
        scratch_shapes=[pltpu.VMEM((tm, n), F32)],
        compiler_params=_params("parallel", "arbitrary"),
        name=name,
    )(a, w, res, g.reshape(1, n).astype(F32))


def _repack_kernel(wm_ref, wn_ref, o_ref, *, first_shifted, first_halved):
    j = pl.program_id(0)

    @pl.when(j < first_shifted)
    def _():
        o_ref[...] = wm_ref[...].T.astype(o_ref.dtype)

    @pl.when(j >= first_shifted)
    def _():
        gain = jnp.where(j >= first_halved, 0.5, 1.0).astype(F32)
        rows = jnp.concatenate([wm_ref[MLA_ROPE:, :], wn_ref[...]], axis=0)
        o_ref[...] = (rows.T * gain).astype(o_ref.dtype)


def repack_w_in(w_in_t, layer, n_lat, n_halved_from, tn=512):
    _, n_in, k = w_in_t.shape
    n_out = n_in - MLA_ROPE
    assert n_out % tn == 0 and n_lat % tn == 0 and n_halved_from % tn == 0 and tn % MLA_ROPE == 0
    per = tn // MLA_ROPE
    return pl.pallas_call(
        functools.partial(_repack_kernel, first_shifted=n_lat // tn, first_halved=n_halved_from // tn),
        grid=(n_out // tn,),
        in_specs=[pl.BlockSpec((None, tn, k), lambda j: (layer, j, 0)),
                  pl.BlockSpec((None, MLA_ROPE, k), lambda j: (layer, per * (j + 1), 0))],
        out_specs=pl.BlockSpec((k, tn), lambda j: (0, j)),
        out_shape=jax.ShapeDtypeStruct((k, n_out), BF16),
        compiler_params=_params("parallel"),
        name="repack_w_in",
    )(w_in_t, w_in_t)


def _rope_proj_kernel(a_ref, wt_ref, o_ref):
    w = wt_ref[...].astype(BF16)
    o_ref[...] = lax.dot_general(a_ref[...], w, (((1,), (1,)), ((), ())),
                                 preferred_element_type=F32).astype(o_ref.dtype)


def rope_proj(h, w_in_t, layer, col0, tm=1024):
    m, k = h.shape
    return pl.pallas_call(
        _rope_proj_kernel,
        grid=(m // tm,),
        in_specs=[pl.BlockSpec((tm, k), lambda i: (i, 0)),
                  pl.BlockSpec((None, LANES, k), lambda i: (layer, col0 // LANES, 0))],
        out_specs=pl.BlockSpec((tm, LANES), lambda i: (i, 0)),
        out_shape=jax.ShapeDtypeStruct((m, LANES), BF16),
        compiler_params=_params("parallel"),
        name="proj_rope",
    )(h, w_in_t)


def _s5_scan_kernel(u_ref, win_ref, pw_ref, cmat_ref, d_ref, y_ref, h_ref, carry_ref, *, n_tiles, half):
    t = pl.program_id(1)

    @pl.when(t == 0)
    def _():
        carry_ref[...] = jnp.zeros_like(carry_ref)

    u = u_ref[...]
    h_ref[...] = jnp.dot(u, win_ref[0], preferred_element_type=F32)

    def tile(i, carry):
        c_re, c_im = carry
        rows = pl.ds(pl.multiple_of(i * SUBLANES, SUBLANES), SUBLANES)
        x_re = h_ref[rows, :half]
        x_im = h_ref[rows, half:]
        for lvl, shift in enumerate((1, 2, 4)):
            a_re = pw_ref[0, (2 * lvl) * SUBLANES:(2 * lvl + 1) * SUBLANES, :]
            a_im = pw_ref[0, (2 * lvl + 1) * SUBLANES:(2 * lvl + 2) * SUBLANES, :]
            s_re = pltpu.roll(x_re, shift, 0)
            s_im = pltpu.roll(x_im, shift, 0)
            x_re, x_im = (x_re + (a_re * s_re - a_im * s_im),
                          x_im + (a_re * s_im + a_im * s_re))
        p_re = pw_ref[0, 6 * SUBLANES:7 * SUBLANES, :]
        p_im = pw_ref[0, 7 * SUBLANES:8 * SUBLANES, :]
        x_re, x_im = (x_re + (p_re * c_re - p_im * c_im),
                      x_im + (p_re * c_im + p_im * c_re))
        h_ref[rows, :half] = x_re
        h_ref[rows, half:] = x_im
        last = SUBLANES - 1
        return (jnp.broadcast_to(x_re[last:, :], x_re.shape),
                jnp.broadcast_to(x_im[last:, :], x_im.shape))

    c_re, c_im = lax.fori_loop(0, n_tiles, tile, (carry_ref[0], carry_ref[1]), unroll=8)
    carry_ref[0] = c_re
    carry_ref[1] = c_im

    y = jnp.dot(h_ref[...].astype(BF16), cmat_ref[0], preferred_element_type=F32)
    y = y + d_ref[...] * u.astype(F32)
    c0 = math.sqrt(2.0 / math.pi)
    y = 0.5 * y * (1.0 + jnp.tanh(c0 * (y + 0.044715 * (y * y * y))))
    y_ref[...] = y.astype(y_ref.dtype)


def s5_scan(lat, win, pw, cmat, d, tc=1024):
    s = lat.shape[0]
    nblk, _, two_half = win.shape
    half = two_half // 2
    w = nblk * LANES
    kern = functools.partial(_s5_scan_kernel, n_tiles=tc // SUBLANES, half=half)
    return pl.pallas_call(
        kern,
        grid=(nblk, s // tc),
        in_specs=[pl.BlockSpec((tc, LANES), lambda b, t: (t, b)),
                  pl.BlockSpec((1, LANES, two_half), lambda b, t: (b, 0, 0)),
                  pl.BlockSpec((1, 8 * SUBLANES, half), lambda b, t: (b, 0, 0)),
                  pl.BlockSpec((1, two_half, LANES), lambda b, t: (b, 0, 0)),
                  pl.BlockSpec((1, LANES), lambda b, t: (0, b))],
        out_specs=pl.BlockSpec((tc, LANES), lambda b, t: (t, b)),
        out_shape=jax.ShapeDtypeStruct((s, w), BF16),
        scratch_shapes=[pltpu.VMEM((tc, two_half), F32),
                        pltpu.VMEM((2, SUBLANES, half), F32)],
        compiler_params=_params("parallel", "arbitrary"),
        name="s5_scan",
    )(lat, win, pw, cmat, d)


def _s5_tables(lam_re, lam_im, log_dt, b_re, b_im, c_re, c_im, d_skip):
    g, p = lam_re.shape
    hg = b_re.shape[-1]
    gpb = LANES // hg
    nblk = g // gpb
    lr = lam_re.astype(F32)
    li = lam_im.astype(F32)
    dt = jnp.exp(log_dt.astype(F32))[:, None]
    mag = jnp.exp(lr * dt)
    ab_re = mag * jnp.cos(li * dt)
    ab_im = mag * jnp.sin(li * dt)
    den = lr * lr + li * li
    nr = ab_re - 1.0
    ni = ab_im
    coef_re = (nr * lr + ni * li) / den
    coef_im = (ni * lr - nr * li) / den
    br = b_re.astype(F32)
    bi = b_im.astype(F32)
    bb_re = coef_re[..., None] * br - coef_im[..., None] * bi
    bb_im = coef_re[..., None] * bi + coef_im[..., None] * br
    eye = jnp.eye(gpb, dtype=F32)

    def block_diag_in(bb):
        v = jnp.transpose(bb, (0, 2, 1)).reshape(nblk, gpb, hg, p)
        return jnp.einsum("bghp,gk->bghkp", v, eye).reshape(nblk, gpb * hg, gpb * p)

    win = jnp.concatenate([block_diag_in(bb_re), block_diag_in(bb_im)], axis=-1).astype(BF16)

    def block_diag_out(c):
        v = jnp.transpose(c.astype(F32), (0, 2, 1)).reshape(nblk, gpb, p, hg)
        return jnp.einsum("bgph,gk->bgpkh", v, eye).reshape(nblk, gpb * p, gpb * hg)

    cmat = jnp.concatenate([block_diag_out(c_re), -block_diag_out(c_im)], axis=1).astype(BF16)

    pows = [(ab_re, ab_im)]
    for _ in range(SUBLANES - 1):
        qr, qi = pows[-1]
        pows.append((qr * ab_re - qi * ab_im, qr * ab_im + qi * ab_re))
    row = jnp.arange(SUBLANES)[:, None]

    def flat(v):
        return v.reshape(nblk, 1, gpb * p)

    parts = []
    for shift in (1, 2, 4):
        keep = (row >= shift).astype(F32)[None]
        parts.append(flat(pows[shift - 1][0]) * keep)
        parts.append(flat(pows[shift - 1][1]) * keep)
    parts.append(jnp.concatenate([flat(pows[r][0]) for r in range(SUBLANES)], axis=1))
    parts.append(jnp.concatenate([flat(pows[r][1]) for r in range(SUBLANES)], axis=1))
    pw = jnp.concatenate(parts, axis=1)
    return win, pw, cmat, d_skip.astype(F32).reshape(1, -1)


def _glu_kernel(y_ref, w_ref, g_ref, o_ref):
    y = y_ref[...]
    z = jnp.dot(y, w_ref[...], preferred_element_type=F32)
    o_ref[...] = (y.astype(F32) * _sigmoid(z) * _silu(g_ref[...])).astype(o_ref.dtype)


def glu(y, w, proj, gate_col0, tm=512):
    s, n = y.shape
    return pl.pallas_call(
        _glu_kernel,
        grid=(s // tm,),
        in_specs=[pl.BlockSpec((tm, n), lambda i: (i, 0)),
                  pl.BlockSpec((n, n), lambda i: (0, 0)),
                  pl.BlockSpec((tm, n), lambda i: (i, gate_col0 // n))],
        out_specs=pl.BlockSpec((tm, n), lambda i: (i, 0)),
        out_shape=jax.ShapeDtypeStruct((s, n), BF16),
        compiler_params=_params("parallel"),
        name="s5_glu",
    )(y, w, proj)


def _rms(x, g):
    x = x.astype(F32)
    ms = jnp.mean(x * x, axis=-1, keepdims=True)
    return (x * lax.rsqrt(ms + NORM_EPS) * g).astype(BF16)


def _mla_q_kernel(lat_ref, g_ref, w_ref, wrot_ref, cos_ref, sin_ref, q_ref, *, heads, scale):
    h = _rms(lat_ref[...], g_ref[...])
    main = jnp.dot(h, w_ref[...], preferred_element_type=F32)
    rot = jnp.dot(h, wrot_ref[...], preferred_element_type=F32)
    cos = cos_ref[...]
    sin = sin_ref[...]
    for hd in range(heads):
        base = hd * MLA_QK
        q_ref[:, base:base + MLA_NOPE] = (main[:, base:base + MLA_NOPE] * scale).astype(q_ref.dtype)
        r = main[:, base + MLA_NOPE:base + MLA_QK] * cos + rot[:, hd * LANES:(hd + 1) * LANES] * sin
        q_ref[:, base + MLA_NOPE:base + MLA_QK] = (r * scale).astype(q_ref.dtype)


def mla_q_proj(lat, col0, g_q, w_main, w_rot, cos, sin, heads, tm=512):
    s = lat.shape[0]
    rank = w_main.shape[0]
    scale = (MLA_NOPE + MLA_ROPE) ** -0.5 * math.log2(math.e)
    col_blk = col0 // rank
    return pl.pallas_call(
        functools.partial(_mla_q_kernel, heads=heads, scale=scale),
        grid=(s // tm,),
        in_specs=[pl.BlockSpec((tm, rank), lambda i: (i, col_blk)),
                  pl.BlockSpec((1, rank), lambda i: (0, 0)),
                  pl.BlockSpec(w_main.shape, lambda i: (0, 0)),
                  pl.BlockSpec(w_rot.shape, lambda i: (0, 0)),
                  pl.BlockSpec((tm, LANES), lambda i: (i, 0)),
                  pl.BlockSpec((tm, LANES), lambda i: (i, 0))],
        out_specs=pl.BlockSpec((tm, heads * MLA_QK), lambda i: (i, 0)),
        out_shape=jax.ShapeDtypeStruct((s, heads * MLA_QK), BF16),
        compiler_params=_params("parallel"),
        name="mla_q_proj",
    )(lat, g_q, w_main, w_rot, cos, sin)


def _mla_kv_kernel(lat_ref, g_ref, wk_ref, wv_ref, kr_ref, cos_ref, sin_ref, k_ref, v_ref, *, heads):
    h = _rms(lat_ref[...], g_ref[...])
    kn = jnp.dot(h, wk_ref[...], preferred_element_type=F32)
    v_ref[...] = jnp.dot(h, wv_ref[...], preferred_element_type=F32).astype(v_ref.dtype)
    x = kr_ref[...].astype(F32)
    half = MLA_ROPE // 2
    lane = lax.broadcasted_iota(jnp.int32, x.shape, 1)
    rot = jnp.where(lane < half, -pltpu.roll(x, LANES - half, 1), pltpu.roll(x, half, 1))
    kr = (x * cos_ref[...] + rot * sin_ref[...]).astype(k_ref.dtype)
    for hd in range(heads):
        base = hd * MLA_QK
        k_ref[:, base:base + MLA_NOPE] = kn[:, hd * MLA_NOPE:(hd + 1) * MLA_NOPE].astype(k_ref.dtype)
        k_ref[:, base + MLA_NOPE:base + MLA_QK] = kr


def mla_kv_proj(lat, col0, g_kv, w_k, w_v, kr_raw, cos, sin, heads, tm=512):
    s = lat.shape[0]
    rank = w_k.shape[0]
    col_blk = col0 // rank
    return pl.pallas_call(
        functools.partial(_mla_kv_kernel, heads=heads),
        grid=(s // tm,),
        in_specs=[pl.BlockSpec((tm, rank), lambda i: (i, col_blk)),
                  pl.BlockSpec((1, rank), lambda i: (0, 0)),
                  pl.BlockSpec(w_k.shape, lambda i: (0, 0)),
                  pl.BlockSpec(w_v.shape, lambda i: (0, 0)),
                  pl.BlockSpec((tm, LANES), lambda i: (i, 0)),
                  pl.BlockSpec((tm, LANES), lambda i: (i, 0)),
                  pl.BlockSpec((tm, LANES), lambda i: (i, 0))],
        out_specs=[pl.BlockSpec((tm, heads * MLA_QK), lambda i: (i, 0)),
                   pl.BlockSpec((tm, heads * MLA_V), lambda i: (i, 0))],
        out_shape=[jax.ShapeDtypeStruct((s, heads * MLA_QK), BF16),
                   jax.ShapeDtypeStruct((s, heads * MLA_V), BF16)],
        compiler_params=_params("parallel"),
        name="mla_kv_proj",
    )(lat, g_kv, w_k, w_v, kr_raw, cos, sin)


def _mla_attn_kernel(q_ref, k_ref, v_ref, g_ref, o_ref, *, tq, tk, heads_per_step):
    i = pl.program_id(1)
    q = q_ref[...]
    diag_blocks = tq // tk
    heads = range(heads_per_step)
    qk_cols = [slice(hh * MLA_QK, (hh + 1) * MLA_QK) for hh in heads]
    v_cols = [slice(hh * MLA_V, (hh + 1) * MLA_V) for hh in heads]

    def step(j, carry, masked):
        rows = pl.ds(pl.multiple_of(j * tk, tk), tk)
        k = k_ref[rows, :]
        v = v_ref[rows, :]
        ss = [lax.dot_general(q[:, c], k[:, c], (((1,), (1,)), ((), ())), preferred_element_type=F32)
              for c in qk_cols]
        if masked:
            r = lax.broadcasted_iota(jnp.int32, ss[0].shape, 0) + i * tq
            c = lax.broadcasted_iota(jnp.int32, ss[0].shape, 1) + j * tk
            ss = [jnp.where(c <= r, s, -jnp.inf) for s in ss]
        out = []
        ps = []
        for s, (m, l, acc) in zip(ss, carry):
            m_new = jnp.maximum(m, jnp.max(s, axis=-1, keepdims=True))
            alpha = jnp.exp2(m - m_new)
            p = jnp.exp2(s - m_new)
            ps.append(p.astype(BF16))
            out.append((m_new, alpha * l + jnp.sum(p, axis=-1, keepdims=True), alpha * acc))
        pv = [jnp.dot(p, v[:, c], preferred_element_type=F32) for p, c in zip(ps, v_cols)]
        return tuple((m, l, acc + o) for (m, l, acc), o in zip(out, pv))

    init = tuple((jnp.full((tq, 1), -jnp.inf, F32), jnp.zeros((tq, 1), F32), jnp.zeros((tq, MLA_V), F32))
                 for _ in heads)
    carry = lax.fori_loop(0, i * diag_blocks, lambda j, c: step(j, c, False), init)
    for d in range(diag_blocks):
        carry = step(i * diag_blocks + d, carry, True)
    out = jnp.concatenate([acc / l for _, l, acc in carry], axis=-1)
    o_ref[...] = (out * _silu(g_ref[...])).astype(o_ref.dtype)


def mla_attention(q, k, v, proj, gate_col0, heads, tq=1024, tk=1024, heads_per_step=2):
    s = q.shape[0]
    hps = heads_per_step
    gate_blk = gate_col0 // (hps * MLA_V)
    return pl.pallas_call(
        functools.partial(_mla_attn_kernel, tq=tq, tk=tk, heads_per_step=hps),
        grid=(heads // hps, s // tq),
        in_specs=[pl.BlockSpec((tq, hps * MLA_QK), lambda h, i: (i, h)),
                  pl.BlockSpec((s, hps * MLA_QK), lambda h, i: (0, h)),
                  pl.BlockSpec((s, hps * MLA_V), lambda h, i: (0, h)),
                  pl.BlockSpec((tq, hps * MLA_V), lambda h, i: (i, gate_blk + h))],
        out_specs=pl.BlockSpec((tq, hps * MLA_V), lambda h, i: (i, h)),
        out_shape=jax.ShapeDtypeStruct((s, heads * MLA_V), BF16),
        compiler_params=_params("parallel", "arbitrary"),
        name="mla_attention",
    )(q, k, v, proj)


def _sb_attn_kernel(q_ref, k_ref, v_ref, tri_ref, g_ref, o_ref, acc_ref, rest_ref, *, t, scale, heads_per_step):
    i = pl.program_id(1)
    q = q_ref[...]
    tri = tri_ref[...]
    acc_ref[...] = jnp.zeros_like(acc_ref)
    rest_ref[...] = jnp.zeros_like(rest_ref)
    r_idx = lax.broadcasted_iota(jnp.int32, (t, t), 0)
    c_idx = lax.broadcasted_iota(jnp.int32, (t, t), 1)
    strictly_causal = c_idx < r_idx

    def block(j, diagonal):
        rows = pl.ds(pl.multiple_of(j * t, t), t)
        k = k_ref[rows, :]
        v = v_ref[rows, :]
        acc = acc_ref[...]
        heads = range(heads_per_step)
        cols = [slice(hh * SB_HEAD_DIM, (hh + 1) * SB_HEAD_DIM) for hh in heads]
        zs = [lax.dot_general(q[:, c], k[:, c], (((1,), (1,)), ((), ())), preferred_element_type=F32) * scale
              for c in cols]
        log_betas = [jnp.minimum(z, 0.0) - jnp.log(1.0 + jnp.exp(-jnp.abs(z))) for z in zs]
        log_rests = [lb - z for lb, z in zip(log_betas, zs)]
        if diagonal:
            log_rests = [jnp.where(strictly_causal, lr, 0.0) for lr in log_rests]
        his = [lr.astype(BF16) for lr in log_rests]
        los = [(lr - hi.astype(F32)).astype(BF16) for lr, hi in zip(log_rests, his)]
        laters = [jnp.dot(hi, tri, preferred_element_type=F32) for hi in his]
        laters = [la + jnp.dot(lo, tri, preferred_element_type=F32) for la, lo in zip(laters, los)]
        ws = [jnp.exp(lb + la + rest_ref[hh]) for hh, lb, la in zip(heads, log_betas, laters)]
        if diagonal:
            ws = [jnp.where(strictly_causal, w, 0.0) for w in ws]
        outs = [acc[:, c] + jnp.dot(w.astype(BF16), v[:, c], preferred_element_type=F32)
                for c, w in zip(cols, ws)]
        acc_ref[...] = jnp.concatenate(outs, axis=-1)
        rest_max = None
        for hh, lr in zip(heads, log_rests):
            rest = rest_ref[hh] + jnp.sum(lr, axis=-1, keepdims=True)
            rest_ref[hh] = rest
            head_max = jnp.max(rest)
            rest_max = head_max if rest_max is None else jnp.maximum(rest_max, head_max)
        return rest_max

    def cond(carry):
        j, rest_max = carry
        return jnp.logical_and(j >= 0, rest_max > SB_LOG_WEIGHT_FLOOR)

    def body(carry):
        j, _ = carry
        return j - 1, block(j, False)

    lax.while_loop(cond, body, (i - 1, block(i, True)))
    o_ref[...] = (acc_ref[...] * _silu(g_ref[...])).astype(o_ref.dtype)


def sb_attention(proj, col0, gate_col0, heads, t=256, heads_per_step=4):
    s = proj.shape[0]
    width = heads_per_step * SB_HEAD_DIM
    groups = heads // heads_per_step
    base = col0 // width
    tri = (jnp.arange(t)[:, None] > jnp.arange(t)[None, :]).astype(BF16)
    return pl.pallas_call(
        functools.partial(_sb_attn_kernel, t=t, scale=SB_HEAD_DIM ** -0.5, heads_per_step=heads_per_step),
        grid=(groups, s // t),
        in_specs=[pl.BlockSpec((t, width), lambda h, i: (i, base + h)),
                  pl.BlockSpec((s, width), lambda h, i: (0, base + groups + h)),
                  pl.BlockSpec((s, width), lambda h, i: (0, base + 2 * groups + h)),
                  pl.BlockSpec((t, t), lambda h, i: (0, 0)),
                  pl.BlockSpec((t, width), lambda h, i: (i, gate_col0 // width + h))],
        out_specs=pl.BlockSpec((t, width), lambda h, i: (i, h)),
        out_shape=jax.ShapeDtypeStruct((s, heads * SB_HEAD_DIM), BF16),
        scratch_shapes=[pltpu.VMEM((t, width), F32), pltpu.VMEM((heads_per_step, t, 1), F32)],
        compiler_params=_params("parallel", "arbitrary"),
        name="sb_attention",
    )(proj, proj, proj, tri, proj)


def _merge_kernel(y0_ref, y1_ref, y2_ref, m0_ref, m1_ref, m2_ref, w_ref, o_ref):
    acc = None
    for n, (y_ref, m_ref) in enumerate(((y0_ref, m0_ref), (y1_ref, m1_ref), (y2_ref, m2_ref))):
        b = jnp.dot(y_ref[...], w_ref[n], preferred_element_type=F32)
        term = b + b * jnp.tanh(m_ref[...].astype(F32))
        acc = term if acc is None else acc + term
    o_ref[...] = acc.astype(o_ref.dtype)


def gated_merge(ys, proj, merge_col0, half_w_branch, d_model, tm=512, tn=1024):
    s, w = ys[0].shape
    y_spec = pl.BlockSpec((tm, w), lambda i, j: (i, 0))

    def merge_spec(n):
        return pl.BlockSpec((tm, tn), lambda i, j: (i, (merge_col0 + n * d_model) // tn + j))

    return pl.pallas_call(
        _merge_kernel,
        grid=(s // tm, d_model // tn),
        in_specs=[y_spec, y_spec, y_spec,
                  merge_spec(0), merge_spec(1), merge_spec(2),
                  pl.BlockSpec((N_BRANCH, w, tn), lambda i, j: (0, 0, j))],
        out_specs=pl.BlockSpec((tm, tn), lambda i, j: (i, j)),
        out_shape=jax.ShapeDtypeStruct((s, d_model), BF16),
        compiler_params=_params("parallel", "arbitrary"),
        name="gated_merge",
    )(ys[0], ys[1], ys[2], proj, proj, proj, half_w_branch)


def _ple_kernel(h_ref, wg_ref, p_ref, wp_ref, x_ref, g_ref, *out_and_scratch, tn, emit_x):
    if emit_x:
        o_ref, n_ref, row_ref = out_and_scratch
    else:
        n_ref, row_ref = out_and_scratch
    gate = _sigmoid(jnp.dot(h_ref[...], wg_ref[...], preferred_element_type=F32))
    e = jnp.dot(p_ref[...].astype(BF16), wp_ref[...], preferred_element_type=F32)
    x = x_ref[...] + gate * e
    if emit_x:
        o_ref[...] = x
    _store_row_and_norm(x, pl.program_id(1), row_ref, g_ref, n_ref, tn)


def ple_gate(h, w_gate, p, w_proj, x, g_next, norm_dtype, emit_x, tm=512, tn=1024):
    s, d = x.shape
    pd = p.shape[1]
    x_spec = pl.BlockSpec((tm, tn), lambda i, j: (i, j))
    n_spec = pl.BlockSpec((tm, d), lambda i, j: (i, 0))
    n_shape = jax.ShapeDtypeStruct((s, d), norm_dtype)
    return pl.pallas_call(
        functools.partial(_ple_kernel, tn=tn, emit_x=emit_x),
        grid=(s // tm, d // tn),
        in_specs=[pl.BlockSpec((tm, d), lambda i, j: (i, 0)),
                  pl.BlockSpec((d, tn), lambda i, j: (0, j)),
                  pl.BlockSpec((tm, pd), lambda i, j: (i, 0)),
                  pl.BlockSpec((pd, tn), lambda i, j: (0, j)),
                  x_spec,
                  pl.BlockSpec((1, d), lambda i, j: (0, 0))],
        out_specs=[x_spec, n_spec] if emit_x else [n_spec],
        out_shape=[jax.ShapeDtypeStruct((s, d), F32), n_shape] if emit_x else [n_shape],
        scratch_shapes=[pltpu.VMEM((tm, d), F32)],
        compiler_params=_params("parallel", "arbitrary"),
        name="ple_gate",
    )(h, w_gate, p, w_proj, x, g_next.reshape(1, d).astype(F32))


def _rot_half_cols(w):
    half = MLA_ROPE // 2
    return jnp.concatenate([-w[..., half:], w[..., :half]], axis=-1)


def _mla_weights(w_uq, w_ukv, heads):
    rank = w_uq.shape[0]
    wq = w_uq.reshape(rank, heads, MLA_NOPE + MLA_ROPE)
    zeros = jnp.zeros((rank, heads, MLA_QK - MLA_NOPE - MLA_ROPE), w_uq.dtype)
    w_main = jnp.concatenate([wq, zeros], axis=-1).reshape(rank, heads * MLA_QK).astype(BF16)
    w_rot = jnp.concatenate([_rot_half_cols(wq[..., MLA_NOPE:]), zeros], axis=-1)
    w_rot = w_rot.reshape(rank, heads * LANES).astype(BF16)
    wkv = w_ukv.reshape(w_ukv.shape[0], heads, MLA_NOPE + MLA_V)
    w_k = wkv[..., :MLA_NOPE].reshape(-1, heads * MLA_NOPE).astype(BF16)
    w_v = wkv[..., MLA_NOPE:].reshape(-1, heads * MLA_V).astype(BF16)
    return w_main, w_rot, w_k, w_v


def _layer(x, h, p_i, w_in_t, layer, s5, w_glu, g_q, g_kv, w_uq, w_ukv, w_branch, w_out, ple_g, w_ple_gate,
           w_ple_proj, cos, sin, g_next, last):
    s, d = x.shape
    w = d // 2
    q_rank = g_q.shape[0]
    kv_rank = g_kv.shape[0]
    heads = w // MLA_V
    n_lat = w + q_rank + kv_rank

    gate_col0 = n_lat + 3 * w
    merge_col0 = gate_col0 + N_BRANCH * w
    w_packed = repack_w_in(w_in_t, layer, n_lat, merge_col0)
    proj = matmul(h, w_packed, BF16, tm=1024, tn=2048, name="proj_in")
    kr_raw = rope_proj(h, w_in_t, layer, n_lat)

    win, pw, cmat, d_skip = s5
    y_ssm = glu(s5_scan(proj, win, pw, cmat, d_skip), w_glu.astype(BF16), proj, gate_col0)

    w_main, w_rot, w_k, w_v = _mla_weights(w_uq, w_ukv, heads)
    q = mla_q_proj(proj, w, g_q.reshape(1, -1).astype(F32), w_main, w_rot, cos, sin, heads)
    k, v = mla_kv_proj(proj, w + q_rank, g_kv.reshape(1, -1).astype(F32), w_k, w_v, kr_raw, cos, sin, heads)
    y_mla = mla_attention(q, k, v, proj, gate_col0 + w, heads)

    y_sb = sb_attention(proj, n_lat, gate_col0 + 2 * w, heads)

    merged = gated_merge((y_ssm, y_mla, y_sb), proj, merge_col0, (0.5 * w_branch).astype(BF16), d)
    x, hp = matmul_residual_norm(merged, w_out.astype(BF16), x, ple_g, tm=512, tn=1024, name="out_proj")

    outs = ple_gate(hp, w_ple_gate.astype(BF16), p_i, w_ple_proj.astype(BF16), x, g_next,
                    F32 if last else BF16, emit_x=not last)
    return (None, outs[0]) if last else (outs[0], outs[1])


def kernel(x, p, ln_g, w_in, ssm_lam_re, ssm_lam_im, ssm_log_dt, ssm_b_re, ssm_b_im, ssm_c_re, ssm_c_im,
           ssm_d, ssm_w_glu, mla_g_q, mla_g_kv, mla_w_uq, mla_w_ukv, w_branch, w_out, ple_g, w_ple_gate,
           w_ple_proj, final_g):
    bsz, seqlen, d = x.shape
    depth = w_in.shape[0]
    pos = jnp.arange(seqlen, dtype=F32)
    inv_freq = ROPE_THETA ** (-jnp.arange(0, MLA_ROPE, 2, dtype=F32) / MLA_ROPE)
    ang = pos[:, None] * inv_freq[None, :]
    pad = jnp.zeros((seqlen, LANES - MLA_ROPE), F32)
    cos = jnp.concatenate([jnp.cos(ang), jnp.cos(ang), pad], axis=-1)
    sin = jnp.concatenate([jnp.sin(ang), jnp.sin(ang), pad], axis=-1)
    w_in_t = jnp.swapaxes(w_in, 1, 2)

    outs = []
    for b in range(bsz):
        xb = x[b]
        hb = rmsnorm(xb, ln_g[0], BF16)
        for i in range(depth):
            last = i == depth - 1
            s5 = _s5_tables(ssm_lam_re[i], ssm_lam_im[i], ssm_log_dt[i], ssm_b_re[i], ssm_b_im[i],
                            ssm_c_re[i], ssm_c_im[i], ssm_d[i])
            xb, hb = _layer(xb, hb, p[i, b], w_in_t, i, s5, ssm_w_glu[i], mla_g_q[i], mla_g_kv[i],
                            mla_w_uq[i], mla_w_ukv[i], w_branch[i], w_out[i], ple_g[i], w_ple_gate[i],
                            w_ple_proj[i], cos, sin, final_g if last else ln_g[i + 1], last)
        outs.append(hb.astype(x.dtype))
    return jnp.stack(outs, axis=0)
```
